```python
import jax
import jax.numpy as jnp
from jax import lax
import numpy as np

D_MODEL = 4096
BATCH = 4
SEQ = 2048
DEPTH = 1
DEC_BATCH = 128
DEC_SEQ = 8
PAST_LEN = 16384
PAGE_SIZE = 128

MLA_HEADS = 16
Q_RANK = 1024
KV_RANK = 512
NOPE_DIM = 128
ROPE_DIM = 64
V_DIM = 128
ROPE_THETA = 10000.0
MLA_SCALE = (NOPE_DIM + ROPE_DIM) ** -0.5

SB_HEADS = 16
SB_KV_HEADS = 2
SB_HEAD_DIM = 128
SB_GROUP = SB_HEADS // SB_KV_HEADS
SB_SCALE = SB_HEAD_DIM ** -0.5

N_BRANCH = 2
IN_SIZES = (Q_RANK, KV_RANK, ROPE_DIM, SB_HEADS * SB_HEAD_DIM, SB_KV_HEADS * SB_HEAD_DIM, SB_KV_HEADS * SB_HEAD_DIM, N_BRANCH * D_MODEL)
IN_COLS = Q_RANK + KV_RANK + ROPE_DIM + (SB_HEADS + 2 * SB_KV_HEADS) * SB_HEAD_DIM + N_BRANCH * D_MODEL

N_EXPERTS = 32
TOP_K = 4
D_FF = D_MODEL // 2
SWIGLU_LIMIT = 7.0
SWIGLU_ALPHA = 1.702
EXPERT_BLOCK = 128

QBLK = 128
EPS = 1e-6
NEG_INF = -1e30

kernel_name = 'hybrid_mla_stickbreaking_moe_step'


def rms_norm(x, g):
    xf = x.astype(jnp.float32)
    y = xf * lax.rsqrt(jnp.mean(xf * xf, axis=-1, keepdims=True) + EPS)
    return y.astype(x.dtype) * g


def rope_tables(pos):
    inv_freq = 1.0 / (ROPE_THETA ** (jnp.arange(0, ROPE_DIM, 2, dtype=jnp.float32) / ROPE_DIM))
    ang = pos.astype(jnp.float32)[:, None] * inv_freq[None, :]
    return jnp.cos(ang), jnp.sin(ang)


def apply_rope(x, cos, sin):
    x1, x2 = jnp.split(x, 2, axis=-1)
    return jnp.concatenate([x1 * cos - x2 * sin, x1 * sin + x2 * cos], axis=-1).astype(x.dtype)


def split_columns(p):
    offsets = []
    acc = 0
    for size in IN_SIZES[:-1]:
        acc += size
        offsets.append(acc)
    return jnp.split(p, offsets, axis=-1)


def mixer_inputs(h, pos, w_in, b_gate, q_a_norm, w_uq, kv_a_norm, w_ukv):
    b, t, _ = h.shape
    c_q, c_kv, k_rope, sb_q, sb_k, sb_v, gate_logits = split_columns(h @ w_in)
    q = (rms_norm(c_q, q_a_norm) @ w_uq).reshape(b, t, MLA_HEADS, NOPE_DIM + ROPE_DIM)
    cos, sin = rope_tables(pos)
    q_rope = apply_rope(q[..., NOPE_DIM:], cos[:, None, :], sin[:, None, :])
    q_lat = jnp.einsum('bthn,rhn->bthr', q[..., :NOPE_DIM], w_ukv[..., :NOPE_DIM])
    c_kv = rms_norm(c_kv, kv_a_norm)
    k_rope = apply_rope(k_rope, cos, sin)
    sb_q = sb_q.reshape(b, t, SB_HEADS, SB_HEAD_DIM)
    sb_k = sb_k.reshape(b, t, SB_KV_HEADS, SB_HEAD_DIM)
    sb_v = sb_v.reshape(b, t, SB_KV_HEADS, SB_HEAD_DIM)
    gates = jax.nn.sigmoid((gate_logits + b_gate).astype(jnp.float32)).astype(h.dtype)
    gates = gates.reshape(b, t, N_BRANCH, D_MODEL)
    return q_lat, q_rope, c_kv, k_rope, sb_q, sb_k, sb_v, gates


def mla_attend(q_lat, q_rope, q_pos, c_kv, k_rope, k_pos):
    s = (jnp.einsum('qhr,kr->hqk', q_lat, c_kv, preferred_element_type=jnp.float32)
         + jnp.einsum('qhd,kd->hqk', q_rope, k_rope, preferred_element_type=jnp.float32))
    s = jnp.where(k_pos[None, None, :] <= q_pos[None, :, None], s * MLA_SCALE, NEG_INF)
    p = jax.nn.softmax(s, axis=-1)
    return jnp.einsum('hqk,kr->qhr', p.astype(c_kv.dtype), c_kv)


def sb_attend(q, q_pos, k, v, k_pos):
    tq = q.shape[0]
    qg = q.reshape(tq, SB_KV_HEADS, SB_GROUP, SB_HEAD_DIM)
    z = jnp.einsum('qcgd,kcd->cgqk', qg, k, preferred_element_type=jnp.float32) * SB_SCALE
    mask = k_pos[None, :] < q_pos[:, None]
    log_keep = jnp.where(mask, jax.nn.log_sigmoid(-z), 0.0)
    log_after = lax.cumsum(log_keep, axis=3, reverse=True) - log_keep
    a = jnp.where(mask, jnp.exp(jax.nn.log_sigmoid(z) + log_after), 0.0)
    o = jnp.einsum('cgqk,kcd->qcgd', a.astype(v.dtype), v)
    return o.reshape(tq, SB_HEADS, SB_HEAD_DIM)


def prompt_attention(q_lat, q_rope, c_kv, k_rope, sb_q, sb_k, sb_v):
    b, s = c_kv.shape[:2]
    k_pos = jnp.arange(s)
    mla = jax.vmap(mla_attend, in_axes=(0, 0, None, 0, 0, None))
    sb = jax.vmap(sb_attend, in_axes=(0, None, 0, 0, None))

    def block(i):
        start = i * QBLK
        q_pos = start + jnp.arange(QBLK)
        sl = lambda a: lax.dynamic_slice_in_dim(a, start, QBLK, axis=1)
        return (mla(sl(q_lat), sl(q_rope), q_pos, c_kv, k_rope, k_pos),
                sb(sl(sb_q), q_pos, sb_k, sb_v, k_pos))

    o_lat, o_sb = lax.map(block, jnp.arange(s // QBLK))
    unblock = lambda o: jnp.moveaxis(o, 0, 1).reshape((b, s) + o.shape[3:])
    return unblock(o_lat), unblock(o_sb)


def sample_attention(q_lat, q_rope, c_kv, k_rope, sb_q, sb_k, sb_v, q_pos,
                     cache_kv_latent, cache_k_rope, cache_sb_k, cache_sb_v, page_table, layer):
    past_len = page_table.shape[1] * PAGE_SIZE
    k_pos = jnp.arange(past_len + q_lat.shape[1])

    def with_past(cache, pt, new_rows):
        rows = cache[layer, pt]
        return jnp.concatenate([rows.reshape((past_len,) + rows.shape[2:]), new_rows], axis=0)

    def one_seq(args):
        ql, qr, ckv, kr, sq, sk, sv, pt = args
        o_lat = mla_attend(ql, qr, q_pos, with_past(cache_kv_latent, pt, ckv), with_past(cache_k_rope, pt, kr), k_pos)
        o_sb = sb_attend(sq, q_pos, with_past(cache_sb_k, pt, sk), with_past(cache_sb_v, pt, sv), k_pos)
        return o_lat, o_sb

    return lax.map(one_seq, (q_lat, q_rope, c_kv, k_rope, sb_q, sb_k, sb_v, page_table))


def merge_branches(o_lat, o_sb, gates, w_ukv, w_branch_mla, w_branch_sb, w_out):
    b, t = o_lat.shape[:2]
    v_mla = jnp.einsum('bthr,rhd->bthd', o_lat, w_ukv[..., NOPE_DIM:]).reshape(b, t, MLA_HEADS * V_DIM)
    u = (gates[:, :, 0] * (v_mla @ w_branch_mla)
         + gates[:, :, 1] * (o_sb.reshape(b, t, SB_HEADS * SB_HEAD_DIM) @ w_branch_sb))
    return u @ w_out


def moe_ffn(h, layer, w_router, b_router, w_gate_up, b_gate_up, w_down, b_down):
    shp = h.shape
    x = h.reshape(-1, D_MODEL)
    t = x.shape[0]
    logits = (x @ w_router + b_router).astype(jnp.float32)
    top_val, top_idx = lax.top_k(logits, TOP_K)
    gate = jax.nn.softmax(top_val, axis=-1)
    flat_e = top_idx.reshape(-1)
    order = jnp.argsort(flat_e)
    sorted_e = flat_e[order]
    tok = (order // TOP_K).astype(jnp.int32)
    gate_sorted = gate.reshape(-1)[order]
    counts = jnp.bincount(flat_e, length=N_EXPERTS)
    padded = (counts + EXPERT_BLOCK - 1) // EXPERT_BLOCK * EXPERT_BLOCK
    starts = jnp.cumsum(counts) - counts
    pad_ends = jnp.cumsum(padded)
    pad_starts = pad_ends - padded
    dest = pad_starts[sorted_e] + jnp.arange(t * TOP_K) - starts[sorted_e]
    n_blocks = (t * TOP_K + N_EXPERTS * (EXPERT_BLOCK - 1) + EXPERT_BLOCK - 1) // EXPERT_BLOCK
    n_rows = n_blocks * EXPERT_BLOCK
    row_tok = jnp.full((n_rows,), t, jnp.int32).at[dest].set(tok)
    x_pad = jnp.concatenate([x, jnp.zeros((1, D_MODEL), x.dtype)], axis=0)
    xb = x_pad[row_tok].reshape(n_blocks, EXPERT_BLOCK, D_MODEL)
    block_e = jnp.minimum(jnp.searchsorted(pad_ends, jnp.arange(n_blocks) * EXPERT_BLOCK, side='right'), N_EXPERTS - 1)

    def expert_block(args):
        xblk, e = args
        gu = xblk @ w_gate_up[layer, e] + b_gate_up[layer, e]
        g = jnp.minimum(gu[:, :D_FF], SWIGLU_LIMIT)
        u = jnp.clip(gu[:, D_FF:], -SWIGLU_LIMIT, SWIGLU_LIMIT)
        return (g * jax.nn.sigmoid(SWIGLU_ALPHA * g) * (u + 1.0)) @ w_down[layer, e] + b_down[layer, e]

    yb = lax.map(expert_block, (xb, block_e)).reshape(n_rows, D_MODEL)
    y = jax.ops.segment_sum(yb[dest] * gate_sorted[:, None].astype(yb.dtype), tok, num_segments=t)
    return y.reshape(shp).astype(h.dtype)


def setup_inputs(seed: int = 0) -> dict:
    key = jax.random.key(seed)
    ks = jax.random.split(key, 25)
    f32 = jnp.float32
    n_pages = PAST_LEN // PAGE_SIZE
    n_used = DEC_BATCH * n_pages
    n_pool = n_used + max(1, n_used // 4)

    def w(k, shape, fan_in):
        return jax.random.normal(k, shape, f32) * (fan_in ** -0.5)

    def gain(k, shape):
        return 1.0 + 0.05 * jax.random.normal(k, shape, f32)

    def bias(k, shape, scale=0.02):
        return scale * jax.random.normal(k, shape, f32)

    page_table = jax.random.permutation(ks[6], n_pool)[:n_used].reshape(DEC_BATCH, n_pages).astype(jnp.int32)
    return {
        'x_prompt': jax.random.normal(ks[0], (BATCH, SEQ, D_MODEL), f32),
        'x_sample': jax.random.normal(ks[1], (DEC_BATCH, DEC_SEQ, D_MODEL), f32),
        'cache_kv_latent': jax.random.normal(ks[2], (DEPTH, n_pool, PAGE_SIZE, KV_RANK), f32),
        'cache_k_rope': jax.random.normal(ks[3], (DEPTH, n_pool, PAGE_SIZE, ROPE_DIM), f32),
        'cache_sb_k': jax.random.normal(ks[4], (DEPTH, n_pool, PAGE_SIZE, SB_KV_HEADS, SB_HEAD_DIM), f32),
        'cache_sb_v': jax.random.normal(ks[5], (DEPTH, n_pool, PAGE_SIZE, SB_KV_HEADS, SB_HEAD_DIM), f32),
        'page_table': page_table,
        'norm_attn': gain(ks[7], (DEPTH, D_MODEL)),
        'w_in': w(ks[8], (DEPTH, D_MODEL, IN_COLS), D_MODEL),
        'b_gate': bias(ks[9], (DEPTH, N_BRANCH * D_MODEL)),
        'q_a_norm': gain(ks[10], (DEPTH, Q_RANK)),
        'w_uq': w(ks[11], (DEPTH, Q_RANK, MLA_HEADS * (NOPE_DIM + ROPE_DIM)), Q_RANK),
        'kv_a_norm': gain(ks[12], (DEPTH, KV_RANK)),
        'w_ukv': w(ks[13], (DEPTH, KV_RANK, MLA_HEADS, NOPE_DIM + V_DIM), KV_RANK),
        'w_branch_mla': w(ks[14], (DEPTH, MLA_HEADS * V_DIM, D_MODEL), MLA_HEADS * V_DIM),
        'w_branch_sb': w(ks[15], (DEPTH, SB_HEADS * SB_HEAD_DIM, D_MODEL), SB_HEADS * SB_HEAD_DIM),
        'w_out': w(ks[16], (DEPTH, D_MODEL, D_MODEL), D_MODEL),
        'norm_ffn': gain(ks[17], (DEPTH, D_MODEL)),
        'w_router': w(ks[18], (DEPTH, D_MODEL, N_EXPERTS), D_MODEL),
        'b_router': bias(ks[19], (DEPTH, N_EXPERTS), 0.01),
        'w_gate_up': w(ks[20], (DEPTH, N_EXPERTS, D_MODEL, 2 * D_FF), D_MODEL),
        'b_gate_up': bias(ks[21], (DEPTH, N_EXPERTS, 2 * D_FF)),
        'w_down': w(ks[22], (DEPTH, N_EXPERTS, D_FF, D_MODEL), D_FF),
        'b_down': bias(ks[23], (DEPTH, N_EXPERTS, D_MODEL)),
        'norm_final': gain(ks[24], (D_MODEL,)),
    }


def reference(x_prompt, x_sample, cache_kv_latent, cache_k_rope, cache_sb_k, cache_sb_v, page_table,
              norm_attn, w_in, b_gate, q_a_norm, w_uq, kv_a_norm, w_ukv, w_branch_mla, w_branch_sb, w_out,
              norm_ffn, w_router, b_router, w_gate_up, b_gate_up, w_down, b_down, norm_final):
    past_len = page_table.shape[1] * PAGE_SIZE
    pos_p = jnp.arange(x_prompt.shape[1])
    pos_s = past_len + jnp.arange(x_sample.shape[1])
    xp, xs = x_prompt, x_sample
    rows_p, rows_s = [], []
    for l in range(DEPTH):
        proj_w = (w_in[l], b_gate[l], q_a_norm[l], w_uq[l], kv_a_norm[l], w_ukv[l])
        merge_w = (w_ukv[l], w_branch_mla[l], w_branch_sb[l], w_out[l])
        ffn_w = (w_router[l], b_router[l], w_gate_up, b_gate_up, w_down, b_down)

        q_lat, q_rope, c_kv, k_rope, sb_q, sb_k, sb_v, gates = mixer_inputs(rms_norm(xp, norm_attn[l]), pos_p, *proj_w)
        o_lat, o_sb = prompt_attention(q_lat, q_rope, c_kv, k_rope, sb_q, sb_k, sb_v)
        xp = xp + merge_branches(o_lat, o_sb, gates, *merge_w)
        xp = xp + moe_ffn(rms_norm(xp, norm_ffn[l]), l, *ffn_w)
        rows_p.append((c_kv, k_rope, sb_k, sb_v))

        q_lat, q_rope, c_kv, k_rope, sb_q, sb_k, sb_v, gates = mixer_inputs(rms_norm(xs, norm_attn[l]), pos_s, *proj_w)
        o_lat, o_sb = sample_attention(q_lat, q_rope, c_kv, k_rope, sb_q, sb_k, sb_v, pos_s,
                                       cache_kv_latent, cache_k_rope, cache_sb_k, cache_sb_v, page_table, l)
        xs = xs + merge_branches(o_lat, o_sb, gates, *merge_w)
        xs = xs + moe_ffn(rms_norm(xs, norm_ffn[l]), l, *ffn_w)
        rows_s.append((c_kv, k_rope, sb_k, sb_v))

    y_prompt = rms_norm(xp, norm_final)
    y_sample = rms_norm(xs, norm_final)
    kvl_p = jnp.stack([r[0] for r in rows_p])
    kr_p = jnp.stack([r[1] for r in rows_p])
    sbk_p = jnp.stack([r[2] for r in rows_p])
    sbv_p = jnp.stack([r[3] for r in rows_p])
    kvl_s = jnp.stack([r[0] for r in rows_s])
    kr_s = jnp.stack([r[1] for r in rows_s])
    sbk_s = jnp.stack([r[2] for r in rows_s])
    sbv_s = jnp.stack([r[3] for r in rows_s])
    return (y_prompt, y_sample, kvl_p, kr_p, sbk_p, sbv_p, kvl_s, kr_s, sbk_s, sbv_s)
```

```python
import functools

import jax
import jax.numpy as jnp
from jax import lax
from jax.experimental import pallas as pl
from jax.experimental.pallas import tpu as pltpu

TOP_K = 4
SWIGLU_LIMIT = 7.0
SWIGLU_ALPHA = 1.702
EPS = 1e-6
NEG_INF = -1e30
ROPE_THETA = 10000.0

LANES = 128
VMEM_LIMIT = 56 * 1024 * 1024
MOE_BLOCK = 256
DECODE_PAGES = 8

BF = jnp.bfloat16
F32 = jnp.float32


def _cp(*sem):
    return pltpu.CompilerParams(dimension_semantics=sem, vmem_limit_bytes=VMEM_LIMIT)


def _dot(a, b):
    return jnp.dot(a, b, preferred_element_type=F32)


def _dot_nt(a, b):
    return lax.dot_general(a, b, (((1,), (1,)), ((), ())), preferred_element_type=F32)


def _rms(x):
    return x * lax.rsqrt(jnp.mean(x * x, axis=-1, keepdims=True) + EPS)


def _rope128(x, cosf, sinf):
    lane = lax.broadcasted_iota(jnp.int32, x.shape, 1)
    rot = jnp.where((lane & 63) < 32, pltpu.roll(x, 96, 1), pltpu.roll(x, 32, 1))
    return x * cosf + rot * sinf


def _rmsnorm_kernel(x_ref, g_ref, o_ref):
    o_ref[...] = (_rms(x_ref[...]) * g_ref[...]).astype(o_ref.dtype)


def _rmsnorm(x, g, out_dtype, tm):
    t, d = x.shape
    return pl.pallas_call(
        _rmsnorm_kernel,
        grid=(t // tm,),
        in_specs=[pl.BlockSpec((tm, d), lambda i: (i, 0)), pl.BlockSpec((1, d), lambda i: (0, 0))],
        out_specs=pl.BlockSpec((tm, d), lambda i: (i, 0)),
        out_shape=jax.ShapeDtypeStruct((t, d), out_dtype),
        compiler_params=_cp("parallel"), name="rmsnorm",
    )(x, g.reshape(1, d))


def _mm_kernel(*refs, n_extra, epilogue, n_scratch):
    a_ref, w_ref = refs[0], refs[1]
    extras = refs[2:2 + n_extra]
    rest = refs[2 + n_extra:]
    outs = rest[:len(rest) - n_scratch]
    scratch = rest[len(rest) - n_scratch:]
    acc = _dot(a_ref[...], w_ref[...].astype(BF))
    epilogue(acc, extras, outs, scratch)


def _mm(name, a, w, col_block0, n_tiles, tm, tn, epilogue, out_shape, out_specs,
        extras=(), extra_specs=(), scratch_shapes=()):
    m, k = a.shape
    in_specs = [pl.BlockSpec((tm, k), lambda i, j: (i, 0)),
                pl.BlockSpec((k, tn), lambda i, j: (0, col_block0 + j))] + list(extra_specs)
    kern = functools.partial(_mm_kernel, n_extra=len(extras), epilogue=epilogue,
                             n_scratch=len(scratch_shapes))
    return pl.pallas_call(
        kern,
        grid=(m // tm, n_tiles),
        in_specs=in_specs,
        out_specs=out_specs,
        out_shape=out_shape,
        scratch_shapes=list(scratch_shapes),
        compiler_params=_cp("parallel", "arbitrary"), name=name,
    )(a, w, *extras)


def _tile_spec(tm, tn, off=0):
    return pl.BlockSpec((tm, tn), lambda i, j: (i, off + j))


def _row_spec(tm, c):
    return pl.BlockSpec((tm, c), lambda i, j: (i, 0))


def _col_spec(tn, off=0):
    return pl.BlockSpec((1, tn), lambda i, j: (0, off + j))


def _latents(h, w_in, q_a_norm, kv_a_norm, q_rank, kv_rank, tm):
    t = h.shape[0]
    tn = kv_rank
    nq = q_rank // tn

    def epilogue(acc, extras, outs, scratch):
        gq_ref, gkv_ref = extras
        cq_ref, ckv_ref = outs
        buf, = scratch
        j = pl.program_id(1)
        buf[j] = acc

        @pl.when(j == nq)
        def _():
            ssq = sum(jnp.sum(buf[s] * buf[s], axis=-1, keepdims=True) for s in range(nq))
            rs = lax.rsqrt(ssq / q_rank + EPS)
            for s in range(nq):
                cq_ref[:, s * tn:(s + 1) * tn] = (buf[s] * rs * gq_ref[:, s * tn:(s + 1) * tn]).astype(BF)
            ckv_ref[...] = _rms(buf[nq]) * gkv_ref[...]

    return _mm(
        "latents", h, w_in, 0, nq + 1, tm, tn, epilogue,
        out_shape=(jax.ShapeDtypeStruct((t, q_rank), BF), jax.ShapeDtypeStruct((t, kv_rank), F32)),
        out_specs=(_row_spec(tm, q_rank), _row_spec(tm, kv_rank)),
        extras=(q_a_norm.reshape(1, -1), kv_a_norm.reshape(1, -1)),
        extra_specs=(pl.BlockSpec((1, q_rank), lambda i, j: (0, 0)),
                     pl.BlockSpec((1, kv_rank), lambda i, j: (0, 0))),
        scratch_shapes=(pltpu.VMEM((nq + 1, tm, tn), F32),))


def _rope_key(h, w_in, col_block0, cosf, sinf, rope_dim, tm):
    t = h.shape[0]

    def epilogue(acc, extras, outs, scratch):
        cos_ref, sin_ref = extras
        kr_ref, krd_ref = outs
        r = _rope128(acc, cos_ref[...], sin_ref[...])
        kr_ref[...] = r[:, :rope_dim]
        lane = lax.broadcasted_iota(jnp.int32, r.shape, 1)
        krd_ref[...] = jnp.where(lane < rope_dim, r, pltpu.roll(r, rope_dim, 1)).astype(BF)

    return _mm(
        "rope_key", h, w_in, col_block0, 1, tm, LANES, epilogue,
        out_shape=(jax.ShapeDtypeStruct((t, rope_dim), F32), jax.ShapeDtypeStruct((t, LANES), BF)),
        out_specs=(_row_spec(tm, rope_dim), _row_spec(tm, LANES)),
        extras=(cosf, sinf), extra_specs=(_row_spec(tm, LANES), _row_spec(tm, LANES)))


def _plain_proj(a, w, col_block0, n_tiles, tm, tn, out_dtype):
    def epilogue(acc, extras, outs, scratch):
        outs[0][...] = acc.astype(out_dtype)

    return _mm("plain_proj", a, w, col_block0, n_tiles, tm, tn, epilogue,
               out_shape=jax.ShapeDtypeStruct((a.shape[0], n_tiles * tn), out_dtype),
               out_specs=_tile_spec(tm, tn))


def _gates(h, w, col_block0, b_gate, tm, tn):
    n = b_gate.shape[0]

    def epilogue(acc, extras, outs, scratch):
        outs[0][...] = jax.nn.sigmoid(acc + extras[0][...])

    return _mm("gates", h, w, col_block0, n // tn, tm, tn, epilogue,
               out_shape=jax.ShapeDtypeStruct((h.shape[0], n), F32),
               out_specs=_tile_spec(tm, tn),
               extras=(b_gate.reshape(1, n),), extra_specs=(_col_spec(tn),))


def _out_proj(u, w_out, x, tm, tn):
    def epilogue(acc, extras, outs, scratch):
        outs[0][...] = extras[0][...] + acc

    t, d = x.shape
    return _mm("out_proj", u, w_out, 0, d // tn, tm, tn, epilogue,
               out_shape=jax.ShapeDtypeStruct((t, d), F32), out_specs=_tile_spec(tm, tn),
               extras=(x,), extra_specs=(_tile_spec(tm, tn),))


def _q_kernel(a_ref, wn_ref, wr_ref, wk_ref, cos_ref, sin_ref, ql_ref, qr_ref):
    a = a_ref[...]
    for s in range(2):
        qn = _dot(a, wn_ref[s].astype(BF)).astype(BF)
        ql_ref[s] = _dot(qn, wk_ref[s].astype(BF)).astype(BF)
    r = _rope128(_dot(a, wr_ref[...].astype(BF)), cos_ref[...], sin_ref[...])
    lane = lax.broadcasted_iota(jnp.int32, r.shape, 1)
    qr_ref[0] = jnp.where(lane < 64, r, 0.0).astype(BF)
    qr_ref[1] = jnp.where(lane >= 64, r, 0.0).astype(BF)


def _mla_queries(cq, w_nope, w_rope_pair, w_kt, cosf, sinf, tm):
    t, qr = cq.shape
    heads, _, nope = w_nope.shape
    kv_rank = w_kt.shape[2]
    return pl.pallas_call(
        _q_kernel,
        grid=(t // tm, heads // 2),
        in_specs=[pl.BlockSpec((tm, qr), lambda i, p: (i, 0)),
                  pl.BlockSpec((2, qr, nope), lambda i, p: (p, 0, 0)),
                  pl.BlockSpec((None, qr, LANES), lambda i, p: (p, 0, 0)),
                  pl.BlockSpec((2, nope, kv_rank), lambda i, p: (p, 0, 0)),
                  pl.BlockSpec((tm, LANES), lambda i, p: (i, 0)),
                  pl.BlockSpec((tm, LANES), lambda i, p: (i, 0))],
        out_specs=(pl.BlockSpec((2, tm, kv_rank), lambda i, p: (p, i, 0)),
                   pl.BlockSpec((2, tm, LANES), lambda i, p: (p, i, 0))),
        out_shape=(jax.ShapeDtypeStruct((heads, t, kv_rank), BF),
                   jax.ShapeDtypeStruct((heads, t, LANES), BF)),
        compiler_params=_cp("parallel", "arbitrary"), name="mla_queries",
    )(cq, w_nope, w_rope_pair, w_kt, cosf, sinf)


def _vup_kernel(o_ref, w_ref, out_ref):
    out_ref[...] = _dot(o_ref[...], w_ref[...].astype(BF)).astype(out_ref.dtype)


def _value_up(o_lat, w_v, tm):
    heads, t, r = o_lat.shape
    v_dim = w_v.shape[2]
    return pl.pallas_call(
        _vup_kernel,
        grid=(t // tm, heads),
        in_specs=[pl.BlockSpec((None, tm, r), lambda i, h: (h, i, 0)),
                  pl.BlockSpec((None, r, v_dim), lambda i, h: (h, 0, 0))],
        out_specs=pl.BlockSpec((tm, v_dim), lambda i, h: (i, h)),
        out_shape=jax.ShapeDtypeStruct((t, heads * v_dim), BF),
        compiler_params=_cp("parallel", "arbitrary"), name="value_up",
    )(o_lat, w_v)


def _mla_prompt_kernel(q_ref, qr_ref, kv_ref, kr_ref, o_ref, m_ref, l_ref, acc_ref, *, tq, tk, scale):
    i = pl.program_id(1)
    j = pl.program_id(2)
    heads = q_ref.shape[0]
    rows = heads * tq
    j_last = (i * tq + tq - 1) // tk

    @pl.when(j == 0)
    def _():
        m_ref[...] = jnp.full(m_ref.shape, NEG_INF, F32)
        l_ref[...] = jnp.zeros(l_ref.shape, F32)
        acc_ref[...] = jnp.zeros(acc_ref.shape, F32)

    @pl.when(j <= j_last)
    def _():
        q = q_ref[...].reshape(rows, q_ref.shape[2])
        qr = qr_ref[...].reshape(rows, qr_ref.shape[2])
        kv = kv_ref[...].astype(BF)
        s = (_dot_nt(q, kv) + _dot_nt(qr, kr_ref[...])) * scale
        q_pos = i * tq + (lax.broadcasted_iota(jnp.int32, s.shape, 0) & (tq - 1))
        k_pos = j * tk + lax.broadcasted_iota(jnp.int32, s.shape, 1)
        s = jnp.where(k_pos <= q_pos, s, NEG_INF)
        m_prev = m_ref[...]
        m_new = jnp.maximum(m_prev, jnp.max(s, axis=-1, keepdims=True))
        alpha = jnp.exp(m_prev - m_new)
        p = jnp.exp(s - m_new)
        l_ref[...] = alpha * l_ref[...] + jnp.sum(p, axis=-1, keepdims=True)
        acc_ref[...] = alpha * acc_ref[...] + _dot(p.astype(BF), kv)
        m_ref[...] = m_new

    @pl.when(j == j_last)
    def _():
        o = acc_ref[...] * (1.0 / l_ref[...])
        o_ref[...] = o.reshape(o_ref.shape).astype(o_ref.dtype)


def _mla_prompt(q_lat, q_rope, ckv, krd, batch, seq, tq, tk, scale):
    heads, _, r = q_lat.shape
    nq, nk = seq // tq, seq // tk
    assert tq & (tq - 1) == 0

    def kmap(b, i, j):
        return (b * nk + jnp.minimum(j, (i * tq + tq - 1) // tk), 0)

    return pl.pallas_call(
        functools.partial(_mla_prompt_kernel, tq=tq, tk=tk, scale=scale),
        grid=(batch, nq, nk),
        in_specs=[pl.BlockSpec((heads, tq, r), lambda b, i, j: (0, b * nq + i, 0)),
                  pl.BlockSpec((heads, tq, LANES), lambda b, i, j: (0, b * nq + i, 0)),
                  pl.BlockSpec((tk, r), kmap),
                  pl.BlockSpec((tk, LANES), kmap)],
        out_specs=pl.BlockSpec((heads, tq, r), lambda b, i, j: (0, b * nq + i, 0)),
        out_shape=jax.ShapeDtypeStruct((heads, batch * seq, r), BF),
        scratch_shapes=[pltpu.VMEM((heads * tq, 1), F32), pltpu.VMEM((heads * tq, 1), F32),
                        pltpu.VMEM((heads * tq, r), F32)],
        compiler_params=_cp("parallel", "parallel", "arbitrary"), name="mla_prompt",
    )(q_lat, q_rope, ckv, krd)


def _strict_upper(n):
    return (lax.broadcasted_iota(jnp.int32, (n, n), 0) > lax.broadcasted_iota(jnp.int32, (n, n), 1)).astype(BF)


def _sb_block(z, mask, carry, v):
    sp = jnp.log1p(jnp.exp(-jnp.abs(z)))
    log_beta = jnp.minimum(z, 0.0) - sp
    log_keep = -jnp.maximum(z, 0.0) - sp
    if mask is not None:
        log_keep = jnp.where(mask, log_keep, 0.0)
    upper = _strict_upper(z.shape[1])
    hi = log_keep.astype(BF)
    lo = (log_keep - hi.astype(F32)).astype(BF)
    after = _dot(hi, upper) + _dot(lo, upper) + carry
    a = jnp.exp(log_beta + after)
    if mask is not None:
        a = jnp.where(mask, a, 0.0)
    return _dot(a.astype(BF), v), carry + jnp.sum(log_keep, axis=-1, keepdims=True)


def _sb_prompt_kernel(q_ref, k_ref, v_ref, o_ref, qrows_ref, carry_ref, acc_ref, *, tq, tk, group, scale):
    i = pl.program_id(2)
    jj = pl.program_id(3)
    dh = k_ref.shape[1]
    j_max = (i * tq + tq - 1) // tk

    @pl.when(jj == 0)
    def _():
        for g in range(group):
            qrows_ref[g * tq:(g + 1) * tq, :] = q_ref[:, g * dh:(g + 1) * dh]
        carry_ref[...] = jnp.zeros(carry_ref.shape, F32)
        acc_ref[...] = jnp.zeros(acc_ref.shape, F32)

    @pl.when(jj <= j_max)
    def _():
        j = j_max - jj
        z = _dot_nt(qrows_ref[...], k_ref[...].astype(BF)) * scale
        q_pos = i * tq + (lax.broadcasted_iota(jnp.int32, z.shape, 0) & (tq - 1))
        k_pos = j * tk + lax.broadcasted_iota(jnp.int32, z.shape, 1)
        o, carry = _sb_block(z, k_pos < q_pos, carry_ref[...], v_ref[...].astype(BF))
        acc_ref[...] += o
        carry_ref[...] = carry

    @pl.when(jj == j_max)
    def _():
        for g in range(group):
            o_ref[:, g * dh:(g + 1) * dh] = acc_ref[g * tq:(g + 1) * tq, :].astype(o_ref.dtype)


def _sb_prompt(sb_q, sb_k, sb_v, batch, seq, kv_heads, tq, tk, scale):
    dh = sb_k.shape[1] // kv_heads
    group = sb_q.shape[1] // (kv_heads * dh)
    nq, nk = seq // tq, seq // tk
    assert tq & (tq - 1) == 0

    def kmap(b, c, i, jj):
        return (b * nk + jnp.maximum((i * tq + tq - 1) // tk - jj, 0), c)

    return pl.pallas_call(
        functools.partial(_sb_prompt_kernel, tq=tq, tk=tk, group=group, scale=scale),
        grid=(batch, kv_heads, nq, nk),
        in_specs=[pl.BlockSpec((tq, group * dh), lambda b, c, i, jj: (b * nq + i, c)),
                  pl.BlockSpec((tk, dh), kmap),
                  pl.BlockSpec((tk, dh), kmap)],
        out_specs=pl.BlockSpec((tq, group * dh), lambda b, c, i, jj: (b * nq + i, c)),
        out_shape=jax.ShapeDtypeStruct((batch * seq, sb_q.shape[1]), BF),
        scratch_shapes=[pltpu.VMEM((group * tq, dh), BF), pltpu.VMEM((group * tq, 1), F32),
                        pltpu.VMEM((group * tq, dh), F32)],
        compiler_params=_cp("parallel", "parallel", "parallel", "arbitrary"), name="sb_prompt",
    )(sb_q, sb_k, sb_v)


def _mla_decode_kernel(pt_ref, q_ref, qr_ref, *rest, n_pg, dec_seq, scale):
    kv_refs = rest[:n_pg]
    kr_refs = rest[n_pg:2 * n_pg]
    nkv_ref, nkr_ref, o_ref, m_ref, l_ref, acc_ref = rest[2 * n_pg:]
    g = pl.program_id(1)
    q = q_ref[...]
    qr = qr_ref[...]

    @pl.when(g == 0)
    def _():
        m_ref[...] = jnp.full(m_ref.shape, NEG_INF, F32)
        l_ref[...] = jnp.zeros(l_ref.shape, F32)
        acc_ref[...] = jnp.zeros(acc_ref.shape, F32)

    def update(s, kvs):
        m_prev = m_ref[...]
        m_new = jnp.maximum(m_prev, jnp.max(s, axis=-1, keepdims=True))
        alpha = jnp.exp(m_prev - m_new)
        p = jnp.exp(s - m_new)
        l_ref[...] = alpha * l_ref[...] + jnp.sum(p, axis=-1, keepdims=True)
        p = p.astype(BF)
        n = kvs[0].shape[0]
        pv = sum(_dot(p[:, t * n:(t + 1) * n], kvs[t]) for t in range(len(kvs)))
        acc_ref[...] = alpha * acc_ref[...] + pv
        m_ref[...] = m_new

    kvs = [r[...].astype(BF) for r in kv_refs]
    s = jnp.concatenate([_dot_nt(q, kvs[t]) + _dot_nt(qr, kr_refs[t][...].astype(BF)) for t in range(n_pg)],
                        axis=1) * scale
    update(s, kvs)

    @pl.when(g == pl.num_programs(1) - 1)
    def _():
        nkv = nkv_ref[...].astype(BF)
        sn = (_dot_nt(q, nkv) + _dot_nt(qr, nkr_ref[...].astype(BF))) * scale
        row = lax.broadcasted_iota(jnp.int32, sn.shape, 0)
        col = lax.broadcasted_iota(jnp.int32, sn.shape, 1)
        update(jnp.where(col <= (row & (dec_seq - 1)), sn, NEG_INF), [nkv])
        o_ref[...] = (acc_ref[...] * (1.0 / l_ref[...])).astype(o_ref.dtype)


def _mla_decode(page_table, q_s, qr_s, cache_kv, cache_kr, new_kv, new_kr, layer, dec_seq, scale):
    db, rows, r = q_s.shape
    n_pages = page_table.shape[1]
    page = cache_kv.shape[2]
    rope = cache_kr.shape[3]
    n_pg = min(DECODE_PAGES, n_pages)
    assert n_pages % n_pg == 0 and dec_seq & (dec_seq - 1) == 0

    def page_spec(width, t):
        return pl.BlockSpec((None, None, page, width),
                            lambda b, g, pt: (layer, pt[b * n_pages + g * n_pg + t], 0, 0))

    in_specs = ([pl.BlockSpec((None, rows, r), lambda b, g, pt: (b, 0, 0)),
                 pl.BlockSpec((None, rows, rope), lambda b, g, pt: (b, 0, 0))]
                + [page_spec(r, t) for t in range(n_pg)]
                + [page_spec(rope, t) for t in range(n_pg)]
                + [pl.BlockSpec((None, page, r), lambda b, g, pt: (b, 0, 0)),
                   pl.BlockSpec((None, page, rope), lambda b, g, pt: (b, 0, 0))])
    return pl.pallas_call(
        functools.partial(_mla_decode_kernel, n_pg=n_pg, dec_seq=dec_seq, scale=scale),
        grid_spec=pltpu.PrefetchScalarGridSpec(
            num_scalar_prefetch=1,
            grid=(db, n_pages // n_pg),
            in_specs=in_specs,
            out_specs=pl.BlockSpec((None, rows, r), lambda b, g, pt: (b, 0, 0)),
            scratch_shapes=[pltpu.VMEM((rows, 1), F32), pltpu.VMEM((rows, 1), F32), pltpu.VMEM((rows, r), F32)]),
        out_shape=jax.ShapeDtypeStruct((db, rows, r), BF),
        compiler_params=_cp("parallel", "arbitrary"), name="mla_decode",
    )(page_table.reshape(-1), q_s, qr_s, *([cache_kv] * n_pg), *([cache_kr] * n_pg), new_kv, new_kr)


def _sb_decode_kernel(pt_ref, q_ref, *rest, n_pg, dec_seq, kv_heads, scale):
    k_refs = rest[:n_pg]
    v_refs = rest[n_pg:2 * n_pg]
    nk_ref, nv_ref, o_ref, carry_ref, acc_ref = rest[2 * n_pg:]
    g = pl.program_id(1)
    q = q_ref[...]
    page = nk_ref.shape[0] // kv_heads

    def heads_on_lanes(ref):
        return jnp.concatenate([ref[pl.ds(c, page, stride=kv_heads), :] for c in range(kv_heads)], axis=1).astype(BF)

    def step(k, v, mask):
        z = _dot_nt(q, k) * scale
        o, carry = _sb_block(z, mask(z.shape) if mask else None, carry_ref[...], v)
        acc_ref[...] += o
        carry_ref[...] = carry

    @pl.when(g == 0)
    def _():
        carry_ref[...] = jnp.zeros(carry_ref.shape, F32)
        acc_ref[...] = jnp.zeros(acc_ref.shape, F32)

        def new_mask(shape):
            row = lax.broadcasted_iota(jnp.int32, shape, 0)
            col = lax.broadcasted_iota(jnp.int32, shape, 1)
            return col < (row & (dec_seq - 1))

        step(heads_on_lanes(nk_ref), heads_on_lanes(nv_ref), new_mask)

    for t in range(0, n_pg, 2):
        k = jnp.concatenate([heads_on_lanes(k_refs[t + 1]), heads_on_lanes(k_refs[t])], axis=0)
        v = jnp.concatenate([heads_on_lanes(v_refs[t + 1]), heads_on_lanes(v_refs[t])], axis=0)
        step(k, v, None)

    @pl.when(g == pl.num_programs(1) - 1)
    def _():
        rows = acc_ref.shape[0] // kv_heads
        dh = acc_ref.shape[1] // kv_heads
        for c in range(kv_heads):
            o_ref[c] = acc_ref[c * rows:(c + 1) * rows, c * dh:(c + 1) * dh].astype(o_ref.dtype)


def _sb_decode(page_table, q_bd, cache_k, cache_v, new_k, new_v, layer, dec_seq, kv_heads, scale):
    db, rows2, width = q_bd.shape
    dh = width // kv_heads
    n_pages = page_table.shape[1]
    prow = cache_k.shape[2]
    n_pg = min(DECODE_PAGES, n_pages)
    assert n_pages % n_pg == 0 and n_pg % 2 == 0 and dec_seq & (dec_seq - 1) == 0

    def page_spec(t):
        return pl.BlockSpec((None, None, prow, dh),
                            lambda b, g, pt: (layer, pt[b * n_pages + n_pages - 1 - (g * n_pg + t)], 0, 0))

    in_specs = ([pl.BlockSpec((None, rows2, width), lambda b, g, pt: (b, 0, 0))]
                + [page_spec(t) for t in range(n_pg)] + [page_spec(t) for t in range(n_pg)]
                + [pl.BlockSpec((None, prow, dh), lambda b, g, pt: (b, 0, 0)) for _ in range(2)])
    return pl.pallas_call(
        functools.partial(_sb_decode_kernel, n_pg=n_pg, dec_seq=dec_seq, kv_heads=kv_heads, scale=scale),
        grid_spec=pltpu.PrefetchScalarGridSpec(
            num_scalar_prefetch=1,
            grid=(db, n_pages // n_pg),
            in_specs=in_specs,
            out_specs=pl.BlockSpec((None, kv_heads, rows2 // kv_heads, dh), lambda b, g, pt: (b, 0, 0, 0)),
            scratch_shapes=[pltpu.VMEM((rows2, 1), F32), pltpu.VMEM((rows2, width), F32)]),
        out_shape=jax.ShapeDtypeStruct((db, kv_heads, rows2 // kv_heads, dh), BF),
        compiler_params=_cp("parallel", "arbitrary"), name="sb_decode",
    )(page_table.reshape(-1), q_bd, *([cache_k] * n_pg), *([cache_v] * n_pg), new_k, new_v)


def _merge_kernel(vm_ref, os_ref, wm_ref, ws_ref, ga_ref, gb_ref, u_ref):
    ya = _dot(vm_ref[...], wm_ref[...].astype(BF))
    yb = _dot(os_ref[...], ws_ref[...].astype(BF))
    u_ref[...] = (ga_ref[...] * ya + gb_ref[...] * yb).astype(u_ref.dtype)


def _merge(v_mla, o_sb, w_branch_mla, w_branch_sb, gates, tm, tn):
    t, ka = v_mla.shape
    kb = o_sb.shape[1]
    d = w_branch_mla.shape[1]
    nj = d // tn
    return pl.pallas_call(
        _merge_kernel,
        grid=(t // tm, nj),
        in_specs=[pl.BlockSpec((tm, ka), lambda i, j: (i, 0)),
                  pl.BlockSpec((tm, kb), lambda i, j: (i, 0)),
                  pl.BlockSpec((ka, tn), lambda i, j: (0, j)),
                  pl.BlockSpec((kb, tn), lambda i, j: (0, j)),
                  pl.BlockSpec((tm, tn), lambda i, j: (i, j)),
                  pl.BlockSpec((tm, tn), lambda i, j: (i, nj + j))],
        out_specs=pl.BlockSpec((tm, tn), lambda i, j: (i, j)),
        out_shape=jax.ShapeDtypeStruct((t, d), BF),
        compiler_params=_cp("parallel", "arbitrary"), name="merge",
    )(v_mla, o_sb, w_branch_mla, w_branch_sb, gates, gates)


def _router_kernel(x_ref, g_ref, w_ref, b_ref, idx_ref, gate_ref):
    h = _rms(x_ref[...]) * g_ref[...]
    logits = jnp.dot(h, w_ref[...], preferred_element_type=F32, precision=lax.Precision.HIGHEST) + b_ref[...]
    n_exp = logits.shape[1]
    col = lax.broadcasted_iota(jnp.int32, logits.shape, 1)
    slot = lax.broadcasted_iota(jnp.int32, idx_ref.shape, 1)
    idx_out = jnp.zeros(idx_ref.shape, jnp.int32)
    val_out = jnp.zeros(gate_ref.shape, F32)
    top = None
    for k in range(TOP_K):
        m = jnp.max(logits, axis=-1, keepdims=True)
        idx = jnp.min(jnp.where(logits == m, col, n_exp), axis=-1, keepdims=True)
        top = m if top is None else top
        idx_out = jnp.where(slot == k, idx, idx_out)
        val_out = jnp.where(slot == k, jnp.exp(m - top), val_out)
        logits = jnp.where(col == idx, -jnp.inf, logits)
    idx_ref[...] = idx_out
    gate_ref[...] = val_out / jnp.sum(val_out, axis=-1, keepdims=True)


def _router(x, g, w_router, b_router, tm):
    t, d = x.shape
    n_exp = w_router.shape[1]
    return pl.pallas_call(
        _router_kernel,
        grid=(t // tm,),
        in_specs=[pl.BlockSpec((tm, d), lambda i: (i, 0)),
                  pl.BlockSpec((1, d), lambda i: (0, 0)),
                  pl.BlockSpec((d, n_exp), lambda i: (0, 0)),
                  pl.BlockSpec((1, n_exp), lambda i: (0, 0))],
        out_specs=(pl.BlockSpec((tm, TOP_K), lambda i: (i, 0)), pl.BlockSpec((tm, TOP_K), lambda i: (i, 0))),
        out_shape=(jax.ShapeDtypeStruct((t, TOP_K), jnp.int32), jax.ShapeDtypeStruct((t, TOP_K), F32)),
        compiler_params=_cp("parallel"), name="router",
    )(x, g.reshape(1, d), w_router, b_router.reshape(1, n_exp))


def _row_copy(src_hbm, row, dst, dst_row, sem):
    return pltpu.make_async_copy(src_hbm.at[pl.ds(row, 1), :], dst.at[pl.ds(dst_row, 1), :], sem)


def _gather_norm_kernel(tok_ref, x_hbm, g_ref, o_ref, buf, sem):
    tm = buf.shape[0]
    base = pl.program_id(0) * tm

    def issue(r, c):
        _row_copy(x_hbm, tok_ref[base + r], buf, r, sem).start()
        return c

    lax.fori_loop(0, tm, issue, 0)
    pltpu.make_async_copy(x_hbm.at[pl.ds(0, tm), :], buf, sem).wait()
    o_ref[...] = (_rms(buf[...]) * g_ref[...]).astype(o_ref.dtype)


def _gather_norm(x, g, row_tok, tm):
    t, d = x.shape
    n_rows = row_tok.shape[0]
    return pl.pallas_call(
        _gather_norm_kernel,
        grid_spec=pltpu.PrefetchScalarGridSpec(
            num_scalar_prefetch=1,
            grid=(n_rows // tm,),
            in_specs=[pl.BlockSpec(memory_space=pl.ANY), pl.BlockSpec((1, d), lambda i, tok: (0, 0))],
            out_specs=pl.BlockSpec((tm, d), lambda i, tok: (i, 0)),
            scratch_shapes=[pltpu.VMEM((tm, d), F32), pltpu.SemaphoreType.DMA]),
        out_shape=jax.ShapeDtypeStruct((n_rows, d), BF),
        compiler_params=_cp("arbitrary"), name="gather_norm",
    )(row_tok, x, g.reshape(1, d))


def _expert_up_kernel(be_ref, nv_ref, x_ref, wg_ref, wu_ref, bg_ref, bu_ref, h_ref):
    @pl.when(pl.program_id(1) < nv_ref[0])
    def _():
        x = x_ref[...]
        g = _dot(x, wg_ref[...].astype(BF)) + bg_ref[...]
        u = _dot(x, wu_ref[...].astype(BF)) + bu_ref[...]
        g = jnp.minimum(g, SWIGLU_LIMIT)
        u = jnp.clip(u, -SWIGLU_LIMIT, SWIGLU_LIMIT)
        h_ref[...] = (g * jax.nn.sigmoid(SWIGLU_ALPHA * g) * (u + 1.0)).astype(h_ref.dtype)

    @pl.when(pl.program_id(1) >= nv_ref[0])
    def _():
        h_ref[...] = jnp.zeros(h_ref.shape, h_ref.dtype)


def _expert_up(xs, block_e, n_valid, w_gate_up, b_gate_up, layer, tm, tn):
    n_rows, d = xs.shape
    d_ff = w_gate_up.shape[3] // 2
    nj = d_ff // tn
    n_exp = w_gate_up.shape[1]
    b3 = b_gate_up.reshape(b_gate_up.shape[0], n_exp, 1, 2 * d_ff)
    return pl.pallas_call(
        _expert_up_kernel,
        grid_spec=pltpu.PrefetchScalarGridSpec(
            num_scalar_prefetch=2,
            grid=(nj, n_rows // tm),
            in_specs=[pl.BlockSpec((tm, d), lambda j, i, be, nv: (jnp.minimum(i, nv[0] - 1), 0)),
                      pl.BlockSpec((None, None, d, tn), lambda j, i, be, nv: (layer, be[i], 0, j)),
                      pl.BlockSpec((None, None, d, tn), lambda j, i, be, nv: (layer, be[i], 0, nj + j)),
                      pl.BlockSpec((None, None, 1, tn), lambda j, i, be, nv: (layer, be[i], 0, j)),
                      pl.BlockSpec((None, None, 1, tn), lambda j, i, be, nv: (layer, be[i], 0, nj + j))],
            out_specs=pl.BlockSpec((tm, tn), lambda j, i, be, nv: (i, j))),
        out_shape=jax.ShapeDtypeStruct((n_rows, d_ff), BF),
        compiler_params=_cp("parallel", "arbitrary"), name="expert_up",
    )(block_e, n_valid, xs, w_gate_up, w_gate_up, b3, b3)


def _expert_down_kernel(be_ref, nv_ref, h_ref, w_ref, b_ref, y_ref):
    @pl.when(pl.program_id(1) < nv_ref[0])
    def _():
        y_ref[...] = _dot(h_ref[...], w_ref[...].astype(BF)) + b_ref[...]

    @pl.when(pl.program_id(1) >= nv_ref[0])
    def _():
        y_ref[...] = jnp.zeros(y_ref.shape, y_ref.dtype)


def _expert_down(hs, block_e, n_valid, w_down, b_down, layer, tm, tn):
    n_rows, d_ff = hs.shape
    n_exp, _, d = w_down.shape[1:]
    b3 = b_down.reshape(b_down.shape[0], n_exp, 1, d)
    return pl.pallas_call(
        _expert_down_kernel,
        grid_spec=pltpu.PrefetchScalarGridSpec(
            num_scalar_prefetch=2,
            grid=(d // tn, n_rows // tm),
            in_specs=[pl.BlockSpec((tm, d_ff), lambda j, i, be, nv: (jnp.minimum(i, nv[0] - 1), 0)),
                      pl.BlockSpec((None, None, d_ff, tn), lambda j, i, be, nv: (layer, be[i], 0, j)),
                      pl.BlockSpec((None, None, 1, tn), lambda j, i, be, nv: (layer, be[i], 0, j))],
            out_specs=pl.BlockSpec((tm, tn), lambda j, i, be, nv: (i, j))),
        out_shape=jax.ShapeDtypeStruct((n_rows, d), F32),
        compiler_params=_cp("parallel", "arbitrary"), name="expert_down",
    )(block_e, n_valid, hs, w_down, b3)


def _combine_kernel(dest_ref, y_hbm, gate_ref, x_ref, g_ref, o_ref, buf, sem):
    tc = x_ref.shape[0]
    base = pl.program_id(0) * tc * TOP_K

    def issue(r, c):
        _row_copy(y_hbm, dest_ref[base + r], buf, (r % TOP_K) * tc + r // TOP_K, sem).start()
        return c

    lax.fori_loop(0, tc * TOP_K, issue, 0)
    pltpu.make_async_copy(y_hbm.at[pl.ds(0, tc * TOP_K), :], buf, sem).wait()
    gate = gate_ref[...]
    moe = sum(buf[k * tc:(k + 1) * tc, :] * gate[:, k:k + 1] for k in range(TOP_K))
    o_ref[...] = _rms(x_ref[...] + moe) * g_ref[...]


def _combine(y_rows, dest, gate, x, g_final, tc):
    t, d = x.shape
    return pl.pallas_call(
        _combine_kernel,
        grid_spec=pltpu.PrefetchScalarGridSpec(
            num_scalar_prefetch=1,
            grid=(t // tc,),
            in_specs=[pl.BlockSpec(memory_space=pl.ANY),
                      pl.BlockSpec((tc, TOP_K), lambda i, dst: (i, 0)),
                      pl.BlockSpec((tc, d), lambda i, dst: (i, 0)),
                      pl.BlockSpec((1, d), lambda i, dst: (0, 0))],
            out_specs=pl.BlockSpec((tc, d), lambda i, dst: (i, 0)),
            scratch_shapes=[pltpu.VMEM((TOP_K * tc, d), F32), pltpu.SemaphoreType.DMA]),
        out_shape=jax.ShapeDtypeStruct((t, d), F32),
        compiler_params=_cp("arbitrary"), name="combine",
    )(dest, y_rows, gate, x, g_final.reshape(1, d))


def _routing_tables(top_idx, n_exp, tm):
    n_pairs = top_idx.size
    flat_e = top_idx.reshape(-1)
    onehot = (flat_e[:, None] == jnp.arange(n_exp, dtype=jnp.int32)[None, :]).astype(jnp.int32)
    csum = jnp.cumsum(onehot, axis=0)
    counts = csum[-1]
    padded = (counts + tm - 1) // tm * tm
    pad_ends = jnp.cumsum(padded)
    pad_starts = pad_ends - padded
    dest = jnp.sum(onehot * (pad_starts[None, :] + csum - 1), axis=1).astype(jnp.int32)
    n_blocks = (n_pairs + n_exp * (tm - 1) + tm - 1) // tm
    row_tok = jnp.zeros((n_blocks * tm,), jnp.int32).at[dest].set(
        jnp.arange(n_pairs, dtype=jnp.int32) // TOP_K)
    n_valid = (pad_ends[-1] // tm).astype(jnp.int32)
    blk = jnp.arange(n_blocks, dtype=jnp.int32)
    block_e = jnp.searchsorted(pad_ends, jnp.minimum(blk, n_valid - 1) * tm, side='right').astype(jnp.int32)
    return dest, row_tok, jnp.minimum(block_e, n_exp - 1), n_valid.reshape(1)


def _pick(n, pref):
    b = min(pref, n)
    while n % b:
        b //= 2
    return b


def kernel(x_prompt, x_sample, cache_kv_latent, cache_k_rope, cache_sb_k, cache_sb_v, page_table, norm_attn, w_in, b_gate, q_a_norm, w_uq, kv_a_norm, w_ukv, w_branch_mla, w_branch_sb, w_out, norm_ffn, w_router, b_router, w_gate_up, b_gate_up, w_down, b_down, norm_final):
    batch, seq, d = x_prompt.shape
    db, dec_seq, _ = x_sample.shape
    depth = w_in.shape[0]
    q_rank = q_a_norm.shape[1]
    kv_rank = kv_a_norm.shape[1]
    rope = cache_k_rope.shape[3]
    heads = w_ukv.shape[2]
    nope = w_uq.shape[2] // heads - rope
    page = cache_kv_latent.shape[2]
    kv_heads, dh = cache_sb_k.shape[3], cache_sb_k.shape[4]
    n_pages = page_table.shape[1]
    past_len = n_pages * page
    in_cols = w_in.shape[2]
    sb_heads = (in_cols - q_rank - kv_rank - rope - 2 * d) // dh - 2 * kv_heads
    group = sb_heads // kv_heads
    n_exp = w_router.shape[2]
    mla_scale = float(nope + rope) ** -0.5
    sb_scale = float(dh) ** -0.5
    assert rope == 64 and dh == LANES and kv_rank % LANES == 0 and (q_rank + kv_rank) % LANES == 0

    tp, ts = batch * seq, db * dec_seq
    t = tp + ts
    x = jnp.concatenate([x_prompt.reshape(tp, d), x_sample.reshape(ts, d)], axis=0)

    pos = jnp.concatenate([jnp.tile(jnp.arange(seq), batch), jnp.tile(past_len + jnp.arange(dec_seq), db)])
    inv_freq = 1.0 / (ROPE_THETA ** (jnp.arange(0, rope, 2, dtype=F32) / rope))
    ang = pos.astype(F32)[:, None] * inv_freq[None, :]
    cos, sin = jnp.cos(ang), jnp.sin(ang)
    cosf = jnp.concatenate([cos, cos, cos, cos], axis=1)
    sinf = jnp.concatenate([-sin, sin, -sin, sin], axis=1)

    tm_big = _pick(t, 1024)
    tm_mid = _pick(t, 512)
    outs_p, outs_s = [], []
    for l in range(depth):
        rest0 = q_rank + kv_rank + rope
        w_rest = w_in[l][:, rest0:]
        w_uq3 = w_uq[l].reshape(q_rank, heads, nope + rope)
        w_nope = jnp.transpose(w_uq3[:, :, :nope], (1, 0, 2))
        w_rope_pair = jnp.transpose(w_uq3[:, :, nope:].reshape(q_rank, heads // 2, 2 * rope), (1, 0, 2))
        w_kt = jnp.transpose(w_ukv[l][:, :, :nope], (1, 2, 0))
        w_v = jnp.transpose(w_ukv[l][:, :, nope:], (1, 0, 2))

        h = _rmsnorm(x, norm_attn[l], BF, _pick(t, 256))
        cq, ckv = _latents(h, w_in[l], q_a_norm[l], kv_a_norm[l], q_rank, kv_rank, tm_mid)
        kr, krd = _rope_key(h, w_in[l], (q_rank + kv_rank) // LANES, cosf, sinf, rope, tm_big)
        c0 = sb_heads * dh
        tn = _pick(d, 512)
        assert c0 % tn == 0 and (c0 + 2 * kv_heads * dh) % tn == 0 and c0 % (kv_heads * dh) == 0
        sb_q = _plain_proj(h, w_rest, 0, c0 // tn, tm_big, tn, BF)
        sb_kv = _plain_proj(h, w_rest, c0 // (kv_heads * dh), 2, tm_big, kv_heads * dh, F32)
        sb_k, sb_v = sb_kv[:, :kv_heads * dh], sb_kv[:, kv_heads * dh:]
        gates = _gates(h, w_rest, (c0 + 2 * kv_heads * dh) // tn, b_gate[l], tm_big, tn)
        q_lat, q_rope = _mla_queries(cq, w_nope, w_rope_pair, w_kt, cosf, sinf, tm_big)

        o_lat_p = _mla_prompt(q_lat, q_rope, ckv, krd, batch, seq, min(128, seq), min(512, seq), mla_scale)
        o_sb_p = _sb_prompt(sb_q, sb_k, sb_v, batch, seq, kv_heads, min(128, seq), min(256, seq), sb_scale)

        def seq_major(a):
            return jnp.transpose(a[:, tp:].reshape(a.shape[0], db, dec_seq, a.shape[2]), (1, 0, 2, 3)).reshape(
                db, a.shape[0] * dec_seq, a.shape[2])

        def new_page(a, rows_per_pos):
            a = a[tp:].reshape(db, dec_seq * rows_per_pos, a.shape[1] // rows_per_pos)
            return jnp.pad(a, ((0, 0), (0, (page - dec_seq) * rows_per_pos), (0, 0)))

        q_s = seq_major(q_lat)
        qr_m = seq_major(q_rope)
        qr_s = qr_m[:, :, :rope] + qr_m[:, :, rope:]
        o_lat_s = _mla_decode(page_table, q_s, qr_s, cache_kv_latent, cache_k_rope,
                              new_page(ckv, 1), new_page(kr, 1), l, dec_seq, mla_scale)
        o_lat_s = jnp.transpose(o_lat_s.reshape(db, heads, dec_seq, kv_rank), (1, 0, 2, 3)).reshape(heads, ts, kv_rank)

        sbq_s = jnp.transpose(sb_q[tp:].reshape(db, dec_seq, kv_heads, group, dh), (0, 2, 3, 1, 4)).reshape(
            db, kv_heads, group * dec_seq, dh)
        q_bd = jnp.concatenate(
            [jnp.concatenate([sbq_s[:, c] if c2 == c else jnp.zeros_like(sbq_s[:, c]) for c2 in range(kv_heads)], axis=2)
             for c in range(kv_heads)], axis=1)
        cache_k2 = cache_sb_k.reshape(depth, -1, page * kv_heads, dh)
        cache_v2 = cache_sb_v.reshape(depth, -1, page * kv_heads, dh)
        o_sb_s = _sb_decode(page_table, q_bd, cache_k2, cache_v2, new_page(sb_k, kv_heads), new_page(sb_v, kv_heads),
                            l, dec_seq, kv_heads, sb_scale)
        o_sb_s = jnp.transpose(o_sb_s.reshape(db, kv_heads, group, dec_seq, dh), (0, 3, 1, 2, 4)).reshape(ts, sb_heads * dh)

        v_mla = jnp.concatenate([_value_up(o_lat_p, w_v, _pick(tp, 1024)), _value_up(o_lat_s, w_v, _pick(ts, 1024))], axis=0)
        o_sb = jnp.concatenate([o_sb_p, o_sb_s], axis=0)
        u = _merge(v_mla, o_sb, w_branch_mla[l], w_branch_sb[l], gates, tm_mid, tn)
        x = _out_proj(u, w_out[l], x, tm_big, tn)

        top_idx, gate = _router(x, norm_ffn[l], w_router[l], b_router[l], _pick(t, 256))
        dest, row_tok, block_e, n_valid = _routing_tables(top_idx, n_exp, MOE_BLOCK)
        xs = _gather_norm(x, norm_ffn[l], row_tok, MOE_BLOCK)
        hs = _expert_up(xs, block_e, n_valid, w_gate_up, b_gate_up, l, MOE_BLOCK, _pick(w_gate_up.shape[3] // 2, 512))
        ys = _expert_down(hs, block_e, n_valid, w_down, b_down, l, MOE_BLOCK, _pick(d, 1024))
        last = l == depth - 1
        assert last, "a deeper stack needs a combine variant without the final norm"
        y = _combine(ys, dest, gate, x, norm_final, _pick(t, 64))

        outs_p.append((ckv[:tp].reshape(batch, seq, kv_rank), kr[:tp].reshape(batch, seq, rope),
                       sb_k[:tp].reshape(batch, seq, kv_heads, dh), sb_v[:tp].reshape(batch, seq, kv_heads, dh)))
        outs_s.append((ckv[tp:].reshape(db, dec_seq, kv_rank), kr[tp:].reshape(db, dec_seq, rope),
                       sb_k[tp:].reshape(db, dec_seq, kv_heads, dh), sb_v[tp:].reshape(db, dec_seq, kv_heads, dh)))

    y_prompt = y[:tp].reshape(batch, seq, d)
    y_sample = y[tp:].reshape(db, dec_seq, d)
    stack = lambda rows, k: jnp.stack([r[k] for r in rows])
    return (y_prompt, y_sample) + tuple(stack(outs_p, k) for k in range(4)) + tuple(stack(outs_s, k) for k in range(4))
```

```python
import functools

import jax
import jax.numpy as jnp
from jax import lax
from jax.experimental import pallas as pl
from jax.experimental.pallas import tpu as pltpu

TOP_K = 4
SWIGLU_LIMIT = 7.0
SWIGLU_ALPHA = 1.702
EPS = 1e-6
NEG_INF = -1e30
ROPE_THETA = 10000.0

LANES = 128
VMEM_LIMIT = 56 * 1024 * 1024
MOE_BLOCK = 256
DECODE_PAGES = 16

BF = jnp.bfloat16
F32 = jnp.float32


def _cp(*sem):
    return pltpu.CompilerParams(dimension_semantics=sem, vmem_limit_bytes=VMEM_LIMIT)


def _dot(a, b):
    return jnp.dot(a, b, preferred_element_type=F32)


def _dot_nt(a, b):
    return lax.dot_general(a, b, (((1,), (1,)), ((), ())), preferred_element_type=F32)


def _rms(x):
    return x * lax.rsqrt(jnp.mean(x * x, axis=-1, keepdims=True) + EPS)


def _rope128(x, cosf, sinf):
    lane = lax.broadcasted_iota(jnp.int32, x.shape, 1)
    rot = jnp.where((lane & 63) < 32, pltpu.roll(x, 96, 1), pltpu.roll(x, 32, 1))
    return x * cosf + rot * sinf


def _rmsnorm_kernel(x_ref, g_ref, o_ref):
    o_ref[...] = (_rms(x_ref[...]) * g_ref[...]).astype(o_ref.dtype)


def _rmsnorm(x, g, out_dtype, tm):
    t, d = x.shape
    return pl.pallas_call(
        _rmsnorm_kernel,
        grid=(t // tm,),
        in_specs=[pl.BlockSpec((tm, d), lambda i: (i, 0)), pl.BlockSpec((1, d), lambda i: (0, 0))],
        out_specs=pl.BlockSpec((tm, d), lambda i: (i, 0)),
        out_shape=jax.ShapeDtypeStruct((t, d), out_dtype),
        compiler_params=_cp("parallel"), name="rmsnorm",
    )(x, g.reshape(1, d))


def _mm_kernel(*refs, n_extra, epilogue, n_scratch, w_is_nk):
    a_ref, w_ref = refs[0], refs[1]
    extras = refs[2:2 + n_extra]
    rest = refs[2 + n_extra:]
    outs = rest[:len(rest) - n_scratch]
    scratch = rest[len(rest) - n_scratch:]
    w = w_ref[...].astype(BF)
    acc = _dot_nt(a_ref[...], w) if w_is_nk else _dot(a_ref[...], w)
    epilogue(acc, extras, outs, scratch)


def _mm(name, a, w, col_block0, n_tiles, tm, tn, epilogue, out_shape, out_specs,
        extras=(), extra_specs=(), scratch_shapes=(), w_is_nk=False):
    m, k = a.shape
    if w_is_nk:
        w_spec = pl.BlockSpec((tn, k), lambda i, j: (col_block0 + j, 0))
    else:
        w_spec = pl.BlockSpec((k, tn), lambda i, j: (0, col_block0 + j))
    in_specs = [pl.BlockSpec((tm, k), lambda i, j: (i, 0)), w_spec] + list(extra_specs)
    kern = functools.partial(_mm_kernel, n_extra=len(extras), epilogue=epilogue,
                             n_scratch=len(scratch_shapes), w_is_nk=w_is_nk)
    return pl.pallas_call(
        kern,
        grid=(m // tm, n_tiles),
        in_specs=in_specs,
        out_specs=out_specs,
        out_shape=out_shape,
        scratch_shapes=list(scratch_shapes),
        compiler_params=_cp("parallel", "arbitrary"), name=name,
    )(a, w, *extras)


def _tile_spec(tm, tn, off=0):
    return pl.BlockSpec((tm, tn), lambda i, j: (i, off + j))


def _row_spec(tm, c):
    return pl.BlockSpec((tm, c), lambda i, j: (i, 0))


def _col_spec(tn, off=0):
    return pl.BlockSpec((1, tn), lambda i, j: (0, off + j))


def _latents(h, w_in, q_a_norm, kv_a_norm, q_rank, kv_rank, tm):
    t = h.shape[0]
    tn = kv_rank
    nq = q_rank // tn

    def epilogue(acc, extras, outs, scratch):
        gq_ref, gkv_ref = extras
        cq_ref, ckv_ref = outs
        buf, = scratch
        j = pl.program_id(1)
        buf[j] = acc

        @pl.when(j == nq)
        def _():
            ssq = sum(jnp.sum(buf[s] * buf[s], axis=-1, keepdims=True) for s in range(nq))
            rs = lax.rsqrt(ssq / q_rank + EPS)
            for s in range(nq):
                cq_ref[:, s * tn:(s + 1) * tn] = (buf[s] * rs * gq_ref[:, s * tn:(s + 1) * tn]).astype(BF)
            ckv_ref[...] = _rms(buf[nq]) * gkv_ref[...]

    return _mm(
        "latents", h, w_in, 0, nq + 1, tm, tn, epilogue,
        out_shape=(jax.ShapeDtypeStruct((t, q_rank), BF), jax.ShapeDtypeStruct((t, kv_rank), F32)),
        out_specs=(_row_spec(tm, q_rank), _row_spec(tm, kv_rank)),
        extras=(q_a_norm.reshape(1, -1), kv_a_norm.reshape(1, -1)),
        extra_specs=(pl.BlockSpec((1, q_rank), lambda i, j: (0, 0)),
                     pl.BlockSpec((1, kv_rank), lambda i, j: (0, 0))),
        scratch_shapes=(pltpu.VMEM((nq + 1, tm, tn), F32),), w_is_nk=True)


def _rope_key(h, w_in, col_block0, cosf, sinf, rope_dim, tm):
    t = h.shape[0]

    def epilogue(acc, extras, outs, scratch):
        cos_ref, sin_ref = extras
        kr_ref, krd_ref = outs
        r = _rope128(acc, cos_ref[...], sin_ref[...])
        kr_ref[...] = r[:, :rope_dim]
        lane = lax.broadcasted_iota(jnp.int32, r.shape, 1)
        krd_ref[...] = jnp.where(lane < rope_dim, r, pltpu.roll(r, rope_dim, 1)).astype(BF)

    return _mm(
        "rope_key", h, w_in, col_block0, 1, tm, LANES, epilogue,
        out_shape=(jax.ShapeDtypeStruct((t, rope_dim), F32), jax.ShapeDtypeStruct((t, LANES), BF)),
        out_specs=(_row_spec(tm, rope_dim), _row_spec(tm, LANES)),
        extras=(cosf, sinf), extra_specs=(_row_spec(tm, LANES), _row_spec(tm, LANES)), w_is_nk=True)


def _plain_proj(a, w, col_block0, n_tiles, tm, tn, out_dtype):
    def epilogue(acc, extras, outs, scratch):
        outs[0][...] = acc.astype(out_dtype)

    return _mm("plain_proj", a, w, col_block0, n_tiles, tm, tn, epilogue,
               out_shape=jax.ShapeDtypeStruct((a.shape[0], n_tiles * tn), out_dtype),
               out_specs=_tile_spec(tm, tn), w_is_nk=True)


def _gates(h, w, col_block0, b_gate, tm, tn):
    n = b_gate.shape[0]

    def epilogue(acc, extras, outs, scratch):
        outs[0][...] = jax.nn.sigmoid(acc + extras[0][...])

    return _mm("gates", h, w, col_block0, n // tn, tm, tn, epilogue,
               out_shape=jax.ShapeDtypeStruct((h.shape[0], n), F32),
               out_specs=_tile_spec(tm, tn),
               extras=(b_gate.reshape(1, n),), extra_specs=(_col_spec(tn),), w_is_nk=True)


def _out_proj(u, w_out, x, tm, tn):
    def epilogue(acc, extras, outs, scratch):
        outs[0][...] = extras[0][...] + acc

    t, d = x.shape
    return _mm("out_proj", u, w_out, 0, d // tn, tm, tn, epilogue,
               out_shape=jax.ShapeDtypeStruct((t, d), F32), out_specs=_tile_spec(tm, tn),
               extras=(x,), extra_specs=(_tile_spec(tm, tn),))


def _q_kernel(a_ref, wn_ref, wr_ref, wk_ref, cos_ref, sin_ref, ql_ref, qr_ref):
    a = a_ref[...]
    for s in range(2):
        qn = _dot(a, wn_ref[s].astype(BF)).astype(BF)
        ql_ref[s] = _dot(qn, wk_ref[s].astype(BF)).astype(BF)
    r = _rope128(_dot(a, wr_ref[...].astype(BF)), cos_ref[...], sin_ref[...])
    lane = lax.broadcasted_iota(jnp.int32, r.shape, 1)
    qr_ref[0] = jnp.where(lane < 64, r, 0.0).astype(BF)
    qr_ref[1] = jnp.where(lane >= 64, r, 0.0).astype(BF)


def _mla_queries(cq, w_nope, w_rope_pair, w_kt, cosf, sinf, tm):
    t, qr = cq.shape
    heads, _, nope = w_nope.shape
    kv_rank = w_kt.shape[2]
    return pl.pallas_call(
        _q_kernel,
        grid=(t // tm, heads // 2),
        in_specs=[pl.BlockSpec((tm, qr), lambda i, p: (i, 0)),
                  pl.BlockSpec((2, qr, nope), lambda i, p: (p, 0, 0)),
                  pl.BlockSpec((None, qr, LANES), lambda i, p: (p, 0, 0)),
                  pl.BlockSpec((2, nope, kv_rank), lambda i, p: (p, 0, 0)),
                  pl.BlockSpec((tm, LANES), lambda i, p: (i, 0)),
                  pl.BlockSpec((tm, LANES), lambda i, p: (i, 0))],
        out_specs=(pl.BlockSpec((2, tm, kv_rank), lambda i, p: (p, i, 0)),
                   pl.BlockSpec((2, tm, LANES), lambda i, p: (p, i, 0))),
        out_shape=(jax.ShapeDtypeStruct((heads, t, kv_rank), BF),
                   jax.ShapeDtypeStruct((heads, t, LANES), BF)),
        compiler_params=_cp("parallel", "arbitrary"), name="mla_queries",
    )(cq, w_nope, w_rope_pair, w_kt, cosf, sinf)


def _vup_kernel(o_ref, w_ref, out_ref):
    out_ref[...] = _dot(o_ref[...], w_ref[...].astype(BF)).astype(out_ref.dtype)


def _value_up(o_lat, w_v, tm):
    heads, t, r = o_lat.shape
    v_dim = w_v.shape[2]
    return pl.pallas_call(
        _vup_kernel,
        grid=(t // tm, heads),
        in_specs=[pl.BlockSpec((None, tm, r), lambda i, h: (h, i, 0)),
                  pl.BlockSpec((None, r, v_dim), lambda i, h: (h, 0, 0))],
        out_specs=pl.BlockSpec((tm, v_dim), lambda i, h: (i, h)),
        out_shape=jax.ShapeDtypeStruct((t, heads * v_dim), BF),
        compiler_params=_cp("parallel", "arbitrary"), name="value_up",
    )(o_lat, w_v)


def _mla_prompt_kernel(q_ref, qr_ref, kv_ref, kr_ref, o_ref, m_ref, l_ref, acc_ref, *, tq, tk, scale):
    i = pl.program_id(1)
    j = pl.program_id(2)
    heads = q_ref.shape[0]
    rows = heads * tq
    j_last = (i * tq + tq - 1) // tk

    @pl.when(j == 0)
    def _():
        m_ref[...] = jnp.full(m_ref.shape, NEG_INF, F32)
        l_ref[...] = jnp.zeros(l_ref.shape, F32)
        acc_ref[...] = jnp.zeros(acc_ref.shape, F32)

    @pl.when(j <= j_last)
    def _():
        q = q_ref[...].reshape(rows, q_ref.shape[2])
        qr = qr_ref[...].reshape(rows, qr_ref.shape[2])
        kv = kv_ref[...].astype(BF)
        s = (_dot_nt(q, kv) + _dot_nt(qr, kr_ref[...])) * scale
        q_pos = i * tq + (lax.broadcasted_iota(jnp.int32, s.shape, 0) & (tq - 1))
        k_pos = j * tk + lax.broadcasted_iota(jnp.int32, s.shape, 1)
        s = jnp.where(k_pos <= q_pos, s, NEG_INF)
        m_prev = m_ref[...]
        m_new = jnp.maximum(m_prev, jnp.max(s, axis=-1, keepdims=True))
        alpha = jnp.exp(m_prev - m_new)
        p = jnp.exp(s - m_new)
        l_ref[...] = alpha * l_ref[...] + jnp.sum(p, axis=-1, keepdims=True)
        acc_ref[...] = alpha * acc_ref[...] + _dot(p.astype(BF), kv)
        m_ref[...] = m_new

    @pl.when(j == j_last)
    def _():
        o = acc_ref[...] * (1.0 / l_ref[...])
        o_ref[...] = o.reshape(o_ref.shape).astype(o_ref.dtype)


def _mla_prompt(q_lat, q_rope, ckv, krd, batch, seq, tq, tk, scale):
    heads, _, r = q_lat.shape
    nq, nk = seq // tq, seq // tk
    assert tq & (tq - 1) == 0

    def kmap(b, i, j):
        return (b * nk + jnp.minimum(j, (i * tq + tq - 1) // tk), 0)

    return pl.pallas_call(
        functools.partial(_mla_prompt_kernel, tq=tq, tk=tk, scale=scale),
        grid=(batch, nq, nk),
        in_specs=[pl.BlockSpec((heads, tq, r), lambda b, i, j: (0, b * nq + i, 0)),
                  pl.BlockSpec((heads, tq, LANES), lambda b, i, j: (0, b * nq + i, 0)),
                  pl.BlockSpec((tk, r), kmap),
                  pl.BlockSpec((tk, LANES), kmap)],
        out_specs=pl.BlockSpec((heads, tq, r), lambda b, i, j: (0, b * nq + i, 0)),
        out_shape=jax.ShapeDtypeStruct((heads, batch * seq, r), BF),
        scratch_shapes=[pltpu.VMEM((heads * tq, 1), F32), pltpu.VMEM((heads * tq, 1), F32),
                        pltpu.VMEM((heads * tq, r), F32)],
        compiler_params=_cp("parallel", "parallel", "arbitrary"), name="mla_prompt",
    )(q_lat, q_rope, ckv, krd)


def _strict_upper(n):
    return (lax.broadcasted_iota(jnp.int32, (n, n), 0) > lax.broadcasted_iota(jnp.int32, (n, n), 1)).astype(BF)


def _sb_block(z, mask, carry, v):
    sp = jnp.log1p(jnp.exp(-jnp.abs(z)))
    log_beta = jnp.minimum(z, 0.0) - sp
    log_keep = -jnp.maximum(z, 0.0) - sp
    if mask is not None:
        log_keep = jnp.where(mask, log_keep, 0.0)
    upper = _strict_upper(z.shape[1])
    hi = log_keep.astype(BF)
    lo = (log_keep - hi.astype(F32)).astype(BF)
    after = _dot(hi, upper) + _dot(lo, upper) + carry
    a = jnp.exp(log_beta + after)
    if mask is not None:
        a = jnp.where(mask, a, 0.0)
    return _dot(a.astype(BF), v), carry + jnp.sum(log_keep, axis=-1, keepdims=True)


def _sb_unmasked_blocks(z, carry, v, blk):
    n = z.shape[1]
    sp = jnp.log1p(jnp.exp(-jnp.abs(z)))
    log_beta = jnp.minimum(z, 0.0) - sp
    log_keep = -jnp.maximum(z, 0.0) - sp
    upper = _strict_upper(blk)
    cols = [slice(b * blk, (b + 1) * blk) for b in range(n // blk)]
    sums = [jnp.sum(log_keep[:, c], axis=-1, keepdims=True) for c in cols]
    later = carry
    mass_after = [None] * len(cols)
    for b in reversed(range(len(cols))):
        mass_after[b] = later
        later = later + sums[b]
    parts = []
    for b, c in enumerate(cols):
        hi = log_keep[:, c].astype(BF)
        lo = (log_keep[:, c] - hi.astype(F32)).astype(BF)
        after = _dot(hi, upper) + _dot(lo, upper) + mass_after[b]
        parts.append(jnp.exp(log_beta[:, c] + after).astype(BF))
    return _dot(jnp.concatenate(parts, axis=1), v), later


def _sb_prompt_kernel(q_ref, k_ref, v_ref, o_ref, qrows_ref, carry_ref, acc_ref, *, tq, tk, group, scale):
    i = pl.program_id(2)
    jj = pl.program_id(3)
    dh = k_ref.shape[1]
    j_max = (i * tq + tq - 1) // tk

    @pl.when(jj == 0)
    def _():
        for g in range(group):
            qrows_ref[g * tq:(g + 1) * tq, :] = q_ref[:, g * dh:(g + 1) * dh]
        carry_ref[...] = jnp.zeros(carry_ref.shape, F32)
        acc_ref[...] = jnp.zeros(acc_ref.shape, F32)

    @pl.when(jj <= j_max)
    def _():
        j = j_max - jj
        z = _dot_nt(qrows_ref[...], k_ref[...].astype(BF)) * scale
        q_pos = i * tq + (lax.broadcasted_iota(jnp.int32, z.shape, 0) & (tq - 1))
        k_pos = j * tk + lax.broadcasted_iota(jnp.int32, z.shape, 1)
        o, carry = _sb_block(z, k_pos < q_pos, carry_ref[...], v_ref[...].astype(BF))
        acc_ref[...] += o
        carry_ref[...] = carry

    @pl.when(jj == j_max)
    def _():
        for g in range(group):
            o_ref[:, g * dh:(g + 1) * dh] = acc_ref[g * tq:(g + 1) * tq, :].astype(o_ref.dtype)


def _sb_prompt(sb_q, sb_k, sb_v, batch, seq, kv_heads, tq, tk, scale):
    dh = sb_k.shape[1] // kv_heads
    group = sb_q.shape[1] // (kv_heads * dh)
    nq, nk = seq // tq, seq // tk
    assert tq & (tq - 1) == 0

    def kmap(b, c, i, jj):
        return (b * nk + jnp.maximum((i * tq + tq - 1) // tk - jj, 0), c)

    return pl.pallas_call(
        functools.partial(_sb_prompt_kernel, tq=tq, tk=tk, group=group, scale=scale),
        grid=(batch, kv_heads, nq, nk),
        in_specs=[pl.BlockSpec((tq, group * dh), lambda b, c, i, jj: (b * nq + i, c)),
                  pl.BlockSpec((tk, dh), kmap),
                  pl.BlockSpec((tk, dh), kmap)],
        out_specs=pl.BlockSpec((tq, group * dh), lambda b, c, i, jj: (b * nq + i, c)),
        out_shape=jax.ShapeDtypeStruct((batch * seq, sb_q.shape[1]), BF),
        scratch_shapes=[pltpu.VMEM((group * tq, dh), BF), pltpu.VMEM((group * tq, 1), F32),
                        pltpu.VMEM((group * tq, dh), F32)],
        compiler_params=_cp("parallel", "parallel", "parallel", "arbitrary"), name="sb_prompt",
    )(sb_q, sb_k, sb_v)


def _decode_kernel(pt_ref, q_ref, qr_ref, qbd_ref, *rest, n_pg, dec_seq, kv_heads, mla_scale, sb_scale):
    kv_refs = rest[:n_pg]
    kr_refs = rest[n_pg:2 * n_pg]
    k_refs = rest[2 * n_pg:3 * n_pg]
    v_refs = rest[3 * n_pg:4 * n_pg]
    (nkv_ref, nkr_ref, nk_ref, nv_ref, o_ref, osb_ref,
     m_ref, l_ref, acc_ref, carry_ref, sacc_ref) = rest[4 * n_pg:]
    g = pl.program_id(1)
    q = q_ref[...]
    qr = qr_ref[...]
    qbd = qbd_ref[...]
    page = nk_ref.shape[0] // kv_heads

    def heads_on_lanes(ref):
        return jnp.concatenate([ref[pl.ds(c, page, stride=kv_heads), :] for c in range(kv_heads)], axis=1).astype(BF)

    def sb_step(k, v, mask):
        z = _dot_nt(qbd, k) * sb_scale
        o, carry = _sb_block(z, mask, carry_ref[...], v)
        sacc_ref[...] += o
        carry_ref[...] = carry

    @pl.when(g == 0)
    def _():
        m_ref[...] = jnp.full(m_ref.shape, NEG_INF, F32)
        l_ref[...] = jnp.zeros(l_ref.shape, F32)
        acc_ref[...] = jnp.zeros(acc_ref.shape, F32)
        carry_ref[...] = jnp.zeros(carry_ref.shape, F32)
        sacc_ref[...] = jnp.zeros(sacc_ref.shape, F32)
        shape = (qbd.shape[0], page)
        row = lax.broadcasted_iota(jnp.int32, shape, 0)
        col = lax.broadcasted_iota(jnp.int32, shape, 1)
        sb_step(heads_on_lanes(nk_ref), heads_on_lanes(nv_ref), col < (row & (dec_seq - 1)))

    def update(s, kvs):
        m_prev = m_ref[...]
        m_new = jnp.maximum(m_prev, jnp.max(s, axis=-1, keepdims=True))
        alpha = jnp.exp(m_prev - m_new)
        p = jnp.exp(s - m_new)
        l_ref[...] = alpha * l_ref[...] + jnp.sum(p, axis=-1, keepdims=True)
        p = p.astype(BF)
        n = kvs[0].shape[0]
        pv = sum(_dot(p[:, t * n:(t + 1) * n], kvs[t]) for t in range(len(kvs)))
        acc_ref[...] = alpha * acc_ref[...] + pv
        m_ref[...] = m_new

    kvs = [r[...].astype(BF) for r in kv_refs]
    s = jnp.concatenate([_dot_nt(q, kvs[t]) + _dot(qr, kr_refs[t][...].astype(BF)) for t in range(n_pg)],
                        axis=1) * mla_scale
    update(s, kvs)

    k = jnp.concatenate([heads_on_lanes(k_refs[t]) for t in reversed(range(n_pg))], axis=0)
    v = jnp.concatenate([heads_on_lanes(v_refs[t]) for t in reversed(range(n_pg))], axis=0)
    o, carry = _sb_unmasked_blocks(_dot_nt(qbd, k) * sb_scale, carry_ref[...], v, 2 * page)
    sacc_ref[...] += o
    carry_ref[...] = carry

    @pl.when(g == pl.num_programs(1) - 1)
    def _():
        nkv = nkv_ref[...].astype(BF)
        sn = (_dot_nt(q, nkv) + _dot(qr, nkr_ref[...].astype(BF))) * mla_scale
        row = lax.broadcasted_iota(jnp.int32, sn.shape, 0)
        col = lax.broadcasted_iota(jnp.int32, sn.shape, 1)
        update(jnp.where(col <= (row & (dec_seq - 1)), sn, NEG_INF), [nkv])
        o_ref[...] = (acc_ref[...] * (1.0 / l_ref[...])).astype(o_ref.dtype)
        rows = sacc_ref.shape[0] // kv_heads
        dh = sacc_ref.shape[1] // kv_heads
        for c in range(kv_heads):
            osb_ref[c] = sacc_ref[c * rows:(c + 1) * rows, c * dh:(c + 1) * dh].astype(osb_ref.dtype)


def _decode(page_table, q_s, qr_s, q_bd, cache_kv, cache_kr_t, cache_k, cache_v,
            new_kv, new_kr_t, new_k, new_v, layer, dec_seq, kv_heads, mla_scale, sb_scale):
    db, rows, r = q_s.shape
    rows2, width = q_bd.shape[1:]
    dh = width // kv_heads
    n_pages = page_table.shape[1]
    page = cache_kv.shape[2]
    rope = cache_kr_t.shape[2]
    prow = cache_k.shape[2]
    n_pg = min(DECODE_PAGES, n_pages)
    assert n_pages % n_pg == 0 and n_pg % 2 == 0 and dec_seq & (dec_seq - 1) == 0

    def fwd_page(shape, t):
        return pl.BlockSpec((None, None) + shape, lambda b, g, pt: (layer, pt[b * n_pages + g * n_pg + t], 0, 0))

    def rev_page(t):
        return pl.BlockSpec((None, None, prow, dh),
                            lambda b, g, pt: (layer, pt[b * n_pages + n_pages - 1 - (g * n_pg + t)], 0, 0))

    def per_seq(shape):
        return pl.BlockSpec((None,) + shape, lambda b, g, pt: (b,) + (0,) * len(shape))

    in_specs = ([per_seq((rows, r)), per_seq((rows, rope)), per_seq((rows2, width))]
                + [fwd_page((page, r), t) for t in range(n_pg)]
                + [fwd_page((rope, page), t) for t in range(n_pg)]
                + [rev_page(t) for t in range(n_pg)] + [rev_page(t) for t in range(n_pg)]
                + [per_seq((page, r)), per_seq((rope, page)), per_seq((prow, dh)), per_seq((prow, dh))])
    return pl.pallas_call(
        functools.partial(_decode_kernel, n_pg=n_pg, dec_seq=dec_seq, kv_heads=kv_heads,
                          mla_scale=mla_scale, sb_scale=sb_scale),
        grid_spec=pltpu.PrefetchScalarGridSpec(
            num_scalar_prefetch=1,
            grid=(db, n_pages // n_pg),
            in_specs=in_specs,
            out_specs=(per_seq((rows, r)), per_seq((kv_heads, rows2 // kv_heads, dh))),
            scratch_shapes=[pltpu.VMEM((rows, 1), F32), pltpu.VMEM((rows, 1), F32), pltpu.VMEM((rows, r), F32),
                            pltpu.VMEM((rows2, 1), F32), pltpu.VMEM((rows2, width), F32)]),
        out_shape=(jax.ShapeDtypeStruct((db, rows, r), BF),
                   jax.ShapeDtypeStruct((db, kv_heads, rows2 // kv_heads, dh), BF)),
        compiler_params=_cp("parallel", "arbitrary"), name="decode",
    )(page_table.reshape(-1), q_s, qr_s, q_bd, *([cache_kv] * n_pg), *([cache_kr_t] * n_pg),
      *([cache_k] * n_pg), *([cache_v] * n_pg), new_kv, new_kr_t, new_k, new_v)


def _merge_kernel(vm_ref, os_ref, wm_ref, ws_ref, ga_ref, gb_ref, u_ref):
    ya = _dot(vm_ref[...], wm_ref[...].astype(BF))
    yb = _dot(os_ref[...], ws_ref[...].astype(BF))
    u_ref[...] = (ga_ref[...] * ya + gb_ref[...] * yb).astype(u_ref.dtype)


def _merge(v_mla, o_sb, w_branch_mla, w_branch_sb, gates, tm, tn):
    t, ka = v_mla.shape
    kb = o_sb.shape[1]
    d = w_branch_mla.shape[1]
    nj = d // tn
    return pl.pallas_call(
        _merge_kernel,
        grid=(t // tm, nj),
        in_specs=[pl.BlockSpec((tm, ka), lambda i, j: (i, 0)),
                  pl.BlockSpec((tm, kb), lambda i, j: (i, 0)),
                  pl.BlockSpec((ka, tn), lambda i, j: (0, j)),
                  pl.BlockSpec((kb, tn), lambda i, j: (0, j)),
                  pl.BlockSpec((tm, tn), lambda i, j: (i, j)),
                  pl.BlockSpec((tm, tn), lambda i, j: (i, nj + j))],
        out_specs=pl.BlockSpec((tm, tn), lambda i, j: (i, j)),
        out_shape=jax.ShapeDtypeStruct((t, d), BF),
        compiler_params=_cp("parallel", "arbitrary"), name="merge",
    )(v_mla, o_sb, w_branch_mla, w_branch_sb, gates, gates)


def _router_kernel(x_ref, g_ref, w_ref, b_ref, idx_ref, gate_ref, h_ref):
    h = _rms(x_ref[...]) * g_ref[...]
    h_ref[...] = h
    logits = jnp.dot(h, w_ref[...], preferred_element_type=F32, precision=lax.Precision.HIGHEST) + b_ref[...]
    n_exp = logits.shape[1]
    col = lax.broadcasted_iota(jnp.int32, logits.shape, 1)
    slot = lax.broadcasted_iota(jnp.int32, idx_ref.shape, 1)
    idx_out = jnp.zeros(idx_ref.shape, jnp.int32)
    val_out = jnp.zeros(gate_ref.shape, F32)
    top = None
    for k in range(TOP_K):
        m = jnp.max(logits, axis=-1, keepdims=True)
        idx = jnp.min(jnp.where(logits == m, col, n_exp), axis=-1, keepdims=True)
        top = m if top is None else top
        idx_out = jnp.where(slot == k, idx, idx_out)
        val_out = jnp.where(slot == k, jnp.exp(m - top), val_out)
        logits = jnp.where(col == idx, -jnp.inf, logits)
    idx_ref[...] = idx_out
    gate_ref[...] = val_out / jnp.sum(val_out, axis=-1, keepdims=True)


def _router(x, g, w_router, b_router, tm):
    t, d = x.shape
    n_exp = w_router.shape[1]
    return pl.pallas_call(
        _router_kernel,
        grid=(t // tm,),
        in_specs=[pl.BlockSpec((tm, d), lambda i: (i, 0)),
                  pl.BlockSpec((1, d), lambda i: (0, 0)),
                  pl.BlockSpec((d, n_exp), lambda i: (0, 0)),
                  pl.BlockSpec((1, n_exp), lambda i: (0, 0))],
        out_specs=(pl.BlockSpec((tm, TOP_K), lambda i: (i, 0)), pl.BlockSpec((tm, TOP_K), lambda i: (i, 0)),
                   pl.BlockSpec((tm, d), lambda i: (i, 0))),
        out_shape=(jax.ShapeDtypeStruct((t, TOP_K), jnp.int32), jax.ShapeDtypeStruct((t, TOP_K), F32),
                   jax.ShapeDtypeStruct((t, d), F32)),
        compiler_params=_cp("parallel"), name="router",
    )(x, g.reshape(1, d), w_router, b_router.reshape(1, n_exp))


def _row_copy(src_hbm, row, dst, dst_row, sem):
    return pltpu.make_async_copy(src_hbm.at[pl.ds(row, 1), :], dst.at[pl.ds(dst_row, 1), :], sem)


ROW_CHUNK = 16


def _start_rows(src_hbm, idx_ref, idx0, idx_stride, n, dst, dst_row0, sem):
    def issue(r, c):
        _row_copy(src_hbm, idx_ref[idx0 + r * idx_stride], dst, dst_row0 + r, sem).start()
        return c

    lax.fori_loop(0, n, issue, 0, unroll=8)


def _wait_rows(src_hbm, dst, sem):
    pltpu.make_async_copy(src_hbm.at[pl.ds(0, dst.shape[0]), :], dst, sem).wait()


def _gather_rows_kernel(tok_ref, x_hbm, o_ref, buf, sem):
    tm = buf.shape[1]
    i = pl.program_id(0)
    slot = i % 2

    def start(block, s):
        _start_rows(x_hbm, tok_ref, block * tm, 1, tm, buf.at[s], 0, sem.at[s])

    @pl.when(i == 0)
    def _():
        start(0, 0)

    @pl.when(i + 1 < pl.num_programs(0))
    def _():
        start(i + 1, 1 - slot)

    _wait_rows(x_hbm, buf.at[slot], sem.at[slot])

    def chunk(c, carry):
        rows = pl.ds(pl.multiple_of(c * ROW_CHUNK, ROW_CHUNK), ROW_CHUNK)
        o_ref[rows, :] = buf[slot, rows, :].astype(o_ref.dtype)
        return carry

    lax.fori_loop(0, tm // ROW_CHUNK, chunk, 0, unroll=2)


def _gather_rows(x, row_tok, tm):
    t, d = x.shape
    n_rows = row_tok.shape[0]
    return pl.pallas_call(
        _gather_rows_kernel,
        grid_spec=pltpu.PrefetchScalarGridSpec(
            num_scalar_prefetch=1,
            grid=(n_rows // tm,),
            in_specs=[pl.BlockSpec(memory_space=pl.ANY)],
            out_specs=pl.BlockSpec((tm, d), lambda i, tok: (i, 0)),
            scratch_shapes=[pltpu.VMEM((2, tm, d), F32), pltpu.SemaphoreType.DMA((2,))]),
        out_shape=jax.ShapeDtypeStruct((n_rows, d), BF),
        compiler_params=_cp("arbitrary"), name="gather_rows",
    )(row_tok, x)


def _expert_up_kernel(be_ref, nv_ref, x_ref, wg_ref, wu_ref, bg_ref, bu_ref, h_ref):
    @pl.when(pl.program_id(1) < nv_ref[0])
    def _():
        x = x_ref[...]
        g = _dot(x, wg_ref[...].astype(BF)) + bg_ref[...]
        u = _dot(x, wu_ref[...].astype(BF)) + bu_ref[...]
        g = jnp.minimum(g, SWIGLU_LIMIT)
        u = jnp.clip(u, -SWIGLU_LIMIT, SWIGLU_LIMIT)
        h_ref[...] = (g * jax.nn.sigmoid(SWIGLU_ALPHA * g) * (u + 1.0)).astype(h_ref.dtype)

    @pl.when(pl.program_id(1) >= nv_ref[0])
    def _():
        h_ref[...] = jnp.zeros(h_ref.shape, h_ref.dtype)


def _expert_up(xs, block_e, n_valid, w_gate_up, b_gate_up, layer, tm, tn):
    n_rows, d = xs.shape
    d_ff = w_gate_up.shape[3] // 2
    nj = d_ff // tn
    n_exp = w_gate_up.shape[1]
    b3 = b_gate_up.reshape(b_gate_up.shape[0], n_exp, 1, 2 * d_ff)
    return pl.pallas_call(
        _expert_up_kernel,
        grid_spec=pltpu.PrefetchScalarGridSpec(
            num_scalar_prefetch=2,
            grid=(nj, n_rows // tm),
            in_specs=[pl.BlockSpec((tm, d), lambda j, i, be, nv: (jnp.minimum(i, nv[0] - 1), 0)),
                      pl.BlockSpec((None, None, d, tn), lambda j, i, be, nv: (layer, be[i], 0, j)),
                      pl.BlockSpec((None, None, d, tn), lambda j, i, be, nv: (layer, be[i], 0, nj + j)),
                      pl.BlockSpec((None, None, 1, tn), lambda j, i, be, nv: (layer, be[i], 0, j)),
                      pl.BlockSpec((None, None, 1, tn), lambda j, i, be, nv: (layer, be[i], 0, nj + j))],
            out_specs=pl.BlockSpec((tm, tn), lambda j, i, be, nv: (i, j))),
        out_shape=jax.ShapeDtypeStruct((n_rows, d_ff), BF),
        compiler_params=_cp("parallel", "arbitrary"), name="expert_up",
    )(block_e, n_valid, xs, w_gate_up, w_gate_up, b3, b3)


def _expert_down_kernel(be_ref, nv_ref, h_ref, w_ref, b_ref, y_ref):
    @pl.when(pl.program_id(1) < nv_ref[0])
    def _():
        y_ref[...] = _dot(h_ref[...], w_ref[...].astype(BF)) + b_ref[...]

    @pl.when(pl.program_id(1) >= nv_ref[0])
    def _():
        y_ref[...] = jnp.zeros(y_ref.shape, y_ref.dtype)


def _expert_down(hs, block_e, n_valid, w_down, b_down, layer, tm, tn):
    n_rows, d_ff = hs.shape
    n_exp, _, d = w_down.shape[1:]
    b3 = b_down.reshape(b_down.shape[0], n_exp, 1, d)
    return pl.pallas_call(
        _expert_down_kernel,
        grid_spec=pltpu.PrefetchScalarGridSpec(
            num_scalar_prefetch=2,
            grid=(d // tn, n_rows // tm),
            in_specs=[pl.BlockSpec((tm, d_ff), lambda j, i, be, nv: (jnp.minimum(i, nv[0] - 1), 0)),
                      pl.BlockSpec((None, None, d_ff, tn), lambda j, i, be, nv: (layer, be[i], 0, j)),
                      pl.BlockSpec((None, None, 1, tn), lambda j, i, be, nv: (layer, be[i], 0, j))],
            out_specs=pl.BlockSpec((tm, tn), lambda j, i, be, nv: (i, j))),
        out_shape=jax.ShapeDtypeStruct((n_rows, d), F32),
        compiler_params=_cp("parallel", "arbitrary"), name="expert_down",
    )(block_e, n_valid, hs, w_down, b3)


def _combine_kernel(dest_ref, y_hbm, gate_ref, x_ref, g_ref, o_ref, buf, sem):
    tc = x_ref.shape[0]
    i = pl.program_id(0)
    slot = i % 2

    def start(block, s):
        for k in range(TOP_K):
            _start_rows(y_hbm, dest_ref, block * tc * TOP_K + k, TOP_K, tc, buf.at[s], k * tc, sem.at[s])

    @pl.when(i == 0)
    def _():
        start(0, 0)

    @pl.when(i + 1 < pl.num_programs(0))
    def _():
        start(i + 1, 1 - slot)

    _wait_rows(y_hbm, buf.at[slot], sem.at[slot])

    def chunk(c, carry):
        rows = pl.ds(pl.multiple_of(c * 8, 8), 8)
        gate = gate_ref[rows, :]
        moe = sum(buf[slot, pl.ds(pl.multiple_of(k * tc + c * 8, 8), 8), :] * gate[:, k:k + 1] for k in range(TOP_K))
        o_ref[rows, :] = _rms(x_ref[rows, :] + moe) * g_ref[...]
        return carry

    lax.fori_loop(0, tc // 8, chunk, 0, unroll=2)


def _combine(y_rows, dest, gate, x, g_final, tc):
    t, d = x.shape
    return pl.pallas_call(
        _combine_kernel,
        grid_spec=pltpu.PrefetchScalarGridSpec(
            num_scalar_prefetch=1,
            grid=(t // tc,),
            in_specs=[pl.BlockSpec(memory_space=pl.ANY),
                      pl.BlockSpec((tc, TOP_K), lambda i, dst: (i, 0)),
                      pl.BlockSpec((tc, d), lambda i, dst: (i, 0)),
                      pl.BlockSpec((1, d), lambda i, dst: (0, 0))],
            out_specs=pl.BlockSpec((tc, d), lambda i, dst: (i, 0)),
            scratch_shapes=[pltpu.VMEM((2, TOP_K * tc, d), F32), pltpu.SemaphoreType.DMA((2,))]),
        out_shape=jax.ShapeDtypeStruct((t, d), F32),
        compiler_params=_cp("arbitrary"), name="combine",
    )(dest, y_rows, gate, x, g_final.reshape(1, d))


def _routing_tables(top_idx, n_exp, tm):
    n_pairs = top_idx.size
    flat_e = top_idx.reshape(-1)
    onehot = (flat_e[:, None] == jnp.arange(n_exp, dtype=jnp.int32)[None, :]).astype(jnp.int32)
    csum = jnp.cumsum(onehot, axis=0)
    counts = csum[-1]
    padded = (counts + tm - 1) // tm * tm
    pad_ends = jnp.cumsum(padded)
    pad_starts = pad_ends - padded
    dest = jnp.sum(onehot * (pad_starts[None, :] + csum - 1), axis=1).astype(jnp.int32)
    n_blocks = (n_pairs + n_exp * (tm - 1) + tm - 1) // tm
    row_tok = jnp.zeros((n_blocks * tm,), jnp.int32).at[dest].set(
        jnp.arange(n_pairs, dtype=jnp.int32) // TOP_K)
    n_valid = (pad_ends[-1] // tm).astype(jnp.int32)
    blk = jnp.arange(n_blocks, dtype=jnp.int32)
    blk_row0 = jnp.minimum(blk, n_valid - 1) * tm
    block_e = jnp.sum((pad_ends[None, :] <= blk_row0[:, None]).astype(jnp.int32), axis=1)
    return dest, row_tok, jnp.minimum(block_e, n_exp - 1), n_valid.reshape(1)


def _pick(n, pref):
    b = min(pref, n)
    while n % b:
        b //= 2
    return b


def kernel(x_prompt, x_sample, cache_kv_latent, cache_k_rope, cache_sb_k, cache_sb_v, page_table, norm_attn, w_in, b_gate, q_a_norm, w_uq, kv_a_norm, w_ukv, w_branch_mla, w_branch_sb, w_out, norm_ffn, w_router, b_router, w_gate_up, b_gate_up, w_down, b_down, norm_final):
    batch, seq, d = x_prompt.shape
    db, dec_seq, _ = x_sample.shape
    depth = w_in.shape[0]
    q_rank = q_a_norm.shape[1]
    kv_rank = kv_a_norm.shape[1]
    rope = cache_k_rope.shape[3]
    heads = w_ukv.shape[2]
    nope = w_uq.shape[2] // heads - rope
    page = cache_kv_latent.shape[2]
    kv_heads, dh = cache_sb_k.shape[3], cache_sb_k.shape[4]
    n_pages = page_table.shape[1]
    past_len = n_pages * page
    in_cols = w_in.shape[2]
    sb_heads = (in_cols - q_rank - kv_rank - rope - 2 * d) // dh - 2 * kv_heads
    group = sb_heads // kv_heads
    n_exp = w_router.shape[2]
    mla_scale = float(nope + rope) ** -0.5
    sb_scale = float(dh) ** -0.5
    assert rope == 64 and dh == LANES and kv_rank % LANES == 0 and (q_rank + kv_rank) % LANES == 0

    tp, ts = batch * seq, db * dec_seq
    t = tp + ts
    x = jnp.concatenate([x_prompt.reshape(tp, d), x_sample.reshape(ts, d)], axis=0)

    pos = jnp.concatenate([jnp.tile(jnp.arange(seq), batch), jnp.tile(past_len + jnp.arange(dec_seq), db)])
    inv_freq = 1.0 / (ROPE_THETA ** (jnp.arange(0, rope, 2, dtype=F32) / rope))
    ang = pos.astype(F32)[:, None] * inv_freq[None, :]
    cos, sin = jnp.cos(ang), jnp.sin(ang)
    cosf = jnp.concatenate([cos, cos, cos, cos], axis=1)
    sinf = jnp.concatenate([-sin, sin, -sin, sin], axis=1)

    tm_big = _pick(t, 1024)
    tm_mid = _pick(t, 512)
    outs_p, outs_s = [], []
    for l in range(depth):
        rest0 = q_rank + kv_rank + rope
        w_in_t = jnp.swapaxes(w_in, 1, 2)[l]
        w_rest = w_in_t[rest0:]
        w_uq3 = w_uq[l].reshape(q_rank, heads, nope + rope)
        w_nope = jnp.transpose(w_uq3[:, :, :nope], (1, 0, 2))
        w_rope_pair = jnp.transpose(w_uq3[:, :, nope:].reshape(q_rank, heads // 2, 2 * rope), (1, 0, 2))
        w_kt = jnp.transpose(w_ukv[l][:, :, :nope], (1, 2, 0))
        w_v = jnp.transpose(w_ukv[l][:, :, nope:], (1, 0, 2))

        h = _rmsnorm(x, norm_attn[l], BF, _pick(t, 256))
        cq, ckv = _latents(h, w_in_t, q_a_norm[l], kv_a_norm[l], q_rank, kv_rank, tm_mid)
        kr, krd = _rope_key(h, w_in_t, (q_rank + kv_rank) // LANES, cosf, sinf, rope, tm_big)
        c0 = sb_heads * dh
        tn = _pick(d, 512)
        assert c0 % tn == 0 and (c0 + 2 * kv_heads * dh) % tn == 0 and c0 % (kv_heads * dh) == 0
        sb_q = _plain_proj(h, w_rest, 0, c0 // tn, tm_big, tn, BF)
        sb_kv = _plain_proj(h, w_rest, c0 // (kv_heads * dh), 2, tm_big, kv_heads * dh, F32)
        sb_k, sb_v = sb_kv[:, :kv_heads * dh], sb_kv[:, kv_heads * dh:]
        gates = _gates(h, w_rest, (c0 + 2 * kv_heads * dh) // tn, b_gate[l], tm_big, tn)
        q_lat, q_rope = _mla_queries(cq, w_nope, w_rope_pair, w_kt, cosf, sinf, tm_big)

        o_lat_p = _mla_prompt(q_lat, q_rope, ckv, krd, batch, seq, min(128, seq), min(512, seq), mla_scale)
        o_sb_p = _sb_prompt(sb_q, sb_k, sb_v, batch, seq, kv_heads, min(128, seq), min(256, seq), sb_scale)

        def seq_major(a):
            return jnp.transpose(a[:, tp:].reshape(a.shape[0], db, dec_seq, a.shape[2]), (1, 0, 2, 3)).reshape(
                db, a.shape[0] * dec_seq, a.shape[2])

        def new_page(a, rows_per_pos):
            a = a[tp:].reshape(db, dec_seq * rows_per_pos, a.shape[1] // rows_per_pos)
            return jnp.pad(a, ((0, 0), (0, (page - dec_seq) * rows_per_pos), (0, 0)))

        q_s = seq_major(q_lat)
        qr_m = seq_major(q_rope)
        qr_s = qr_m[:, :, :rope] + qr_m[:, :, rope:]
        sbq_s = jnp.transpose(sb_q[tp:].reshape(db, dec_seq, kv_heads, group, dh), (0, 2, 3, 1, 4)).reshape(
            db, kv_heads, group * dec_seq, dh)
        q_bd = jnp.concatenate(
            [jnp.concatenate([sbq_s[:, c] if c2 == c else jnp.zeros_like(sbq_s[:, c]) for c2 in range(kv_heads)], axis=2)
             for c in range(kv_heads)], axis=1)
        cache_k2 = cache_sb_k.reshape(depth, -1, page * kv_heads, dh)
        cache_v2 = cache_sb_v.reshape(depth, -1, page * kv_heads, dh)
        cache_kr_t = jnp.swapaxes(cache_k_rope, 2, 3)
        o_lat_s, o_sb_s = _decode(page_table, q_s, qr_s, q_bd, cache_kv_latent, cache_kr_t, cache_k2, cache_v2,
                                  new_page(ckv, 1), jnp.swapaxes(new_page(kr, 1), 1, 2),
                                  new_page(sb_k, kv_heads), new_page(sb_v, kv_heads),
                                  l, dec_seq, kv_heads, mla_scale, sb_scale)
        o_lat_s = jnp.transpose(o_lat_s.reshape(db, heads, dec_seq, kv_rank), (1, 0, 2, 3)).reshape(heads, ts, kv_rank)
        o_sb_s = jnp.transpose(o_sb_s.reshape(db, kv_heads, group, dec_seq, dh), (0, 3, 1, 2, 4)).reshape(ts, sb_heads * dh)

        v_mla = jnp.concatenate([_value_up(o_lat_p, w_v, _pick(tp, 1024)), _value_up(o_lat_s, w_v, _pick(ts, 1024))], axis=0)
        o_sb = jnp.concatenate([o_sb_p, o_sb_s], axis=0)
        u = _merge(v_mla, o_sb, w_branch_mla[l], w_branch_sb[l], gates, tm_mid, tn)
        x = _out_proj(u, w_out[l], x, tm_big, tn)

        top_idx, gate, h_ffn = _router(x, norm_ffn[l], w_router[l], b_router[l], _pick(t, 256))
        dest, row_tok, block_e, n_valid = _routing_tables(top_idx, n_exp, MOE_BLOCK)
        xs = _gather_rows(h_ffn, row_tok, MOE_BLOCK)
        hs = _expert_up(xs, block_e, n_valid, w_gate_up, b_gate_up, l, MOE_BLOCK, _pick(w_gate_up.shape[3] // 2, 512))
        ys = _expert_down(hs, block_e, n_valid, w_down, b_down, l, MOE_BLOCK, _pick(d, 1024))
        last = l == depth - 1
        assert last, "a deeper stack needs a combine variant without the final norm"
        y = _combine(ys, dest, gate, x, norm_final, _pick(t, 64))

        outs_p.append((ckv[:tp].reshape(batch, seq, kv_rank), kr[:tp].reshape(batch, seq, rope),
                       sb_k[:tp].reshape(batch, seq, kv_heads, dh), sb_v[:tp].reshape(batch, seq, kv_heads, dh)))
        outs_s.append((ckv[tp:].reshape(db, dec_seq, kv_rank), kr[tp:].reshape(db, dec_seq, rope),
                       sb_k[tp:].reshape(db, dec_seq, kv_heads, dh), sb_v[tp:].reshape(db, dec_seq, kv_heads, dh)))

    y_prompt = y[:tp].reshape(batch, seq, d)
    y_sample = y[tp:].reshape(db, dec_seq, d)
    stack = lambda rows, k: jnp.stack([r[k] for r in rows])
    return (y_prompt, y_sample) + tuple(stack(outs_p, k) for k in range(4)) + tuple(stack(outs_s, k) for k in range(4))
```

```python
import functools

import jax
import jax.numpy as jnp
from jax import lax
from jax.experimental import pallas as pl
from jax.experimental.pallas import tpu as pltpu

TOP_K = 4
SWIGLU_LIMIT = 7.0
SWIGLU_ALPHA = 1.702
EPS = 1e-6
NEG_INF = -1e30
ROPE_THETA = 10000.0

LANES = 128
VMEM_LIMIT = 56 * 1024 * 1024
MOE_BLOCK = 256
DECODE_PAGES = 16

BF = jnp.bfloat16
F32 = jnp.float32


def _cp(*sem):
    return pltpu.CompilerParams(dimension_semantics=sem, vmem_limit_bytes=VMEM_LIMIT)


def _dot(a, b):
    return jnp.dot(a, b, preferred_element_type=F32)


def _dot_nt(a, b):
    return lax.dot_general(a, b, (((1,), (1,)), ((), ())), preferred_element_type=F32)


def _rms(x):
    return x * lax.rsqrt(jnp.mean(x * x, axis=-1, keepdims=True) + EPS)


def _rope128(x, cosf, sinf):
    lane = lax.broadcasted_iota(jnp.int32, x.shape, 1)
    rot = jnp.where((lane & 63) < 32, pltpu.roll(x, 96, 1), pltpu.roll(x, 32, 1))
    return x * cosf + rot * sinf


def _rmsnorm_kernel(x_ref, g_ref, o_ref):
    o_ref[...] = (_rms(x_ref[...]) * g_ref[...]).astype(o_ref.dtype)


def _rmsnorm(x, g, out_dtype, tm):
    t, d = x.shape
    return pl.pallas_call(
        _rmsnorm_kernel,
        grid=(t // tm,),
        in_specs=[pl.BlockSpec((tm, d), lambda i: (i, 0)), pl.BlockSpec((1, d), lambda i: (0, 0))],
        out_specs=pl.BlockSpec((tm, d), lambda i: (i, 0)),
        out_shape=jax.ShapeDtypeStruct((t, d), out_dtype),
        compiler_params=_cp("parallel"), name="rmsnorm",
    )(x, g.reshape(1, d))


def _mm_kernel(*refs, n_extra, epilogue, n_scratch, w_is_nk):
    a_ref, w_ref = refs[0], refs[1]
    extras = refs[2:2 + n_extra]
    rest = refs[2 + n_extra:]
    outs = rest[:len(rest) - n_scratch]
    scratch = rest[len(rest) - n_scratch:]
    w = w_ref[...].astype(BF)
    acc = _dot_nt(a_ref[...], w) if w_is_nk else _dot(a_ref[...], w)
    epilogue(acc, extras, outs, scratch)


def _mm(name, a, w, col_block0, n_tiles, tm, tn, epilogue, out_shape, out_specs,
        extras=(), extra_specs=(), scratch_shapes=(), w_is_nk=False):
    m, k = a.shape
    if w_is_nk:
        w_spec = pl.BlockSpec((tn, k), lambda i, j: (col_block0 + j, 0))
    else:
        w_spec = pl.BlockSpec((k, tn), lambda i, j: (0, col_block0 + j))
    in_specs = [pl.BlockSpec((tm, k), lambda i, j: (i, 0)), w_spec] + list(extra_specs)
    kern = functools.partial(_mm_kernel, n_extra=len(extras), epilogue=epilogue,
                             n_scratch=len(scratch_shapes), w_is_nk=w_is_nk)
    return pl.pallas_call(
        kern,
        grid=(m // tm, n_tiles),
        in_specs=in_specs,
        out_specs=out_specs,
        out_shape=out_shape,
        scratch_shapes=list(scratch_shapes),
        compiler_params=_cp("parallel", "arbitrary"), name=name,
    )(a, w, *extras)


def _tile_spec(tm, tn, off=0):
    return pl.BlockSpec((tm, tn), lambda i, j: (i, off + j))


def _row_spec(tm, c):
    return pl.BlockSpec((tm, c), lambda i, j: (i, 0))


def _col_spec(tn, off=0):
    return pl.BlockSpec((1, tn), lambda i, j: (0, off + j))


def _latents(h, w_in, q_a_norm, kv_a_norm, q_rank, kv_rank, tm):
    t = h.shape[0]
    tn = kv_rank
    nq = q_rank // tn

    def epilogue(acc, extras, outs, scratch):
        gq_ref, gkv_ref = extras
        cq_ref, ckv_ref = outs
        buf, = scratch
        j = pl.program_id(1)
        buf[j] = acc

        @pl.when(j == nq)
        def _():
            ssq = sum(jnp.sum(buf[s] * buf[s], axis=-1, keepdims=True) for s in range(nq))
            rs = lax.rsqrt(ssq / q_rank + EPS)
            for s in range(nq):
                cq_ref[:, s * tn:(s + 1) * tn] = (buf[s] * rs * gq_ref[:, s * tn:(s + 1) * tn]).astype(BF)
            ckv_ref[...] = _rms(buf[nq]) * gkv_ref[...]

    return _mm(
        "latents", h, w_in, 0, nq + 1, tm, tn, epilogue,
        out_shape=(jax.ShapeDtypeStruct((t, q_rank), BF), jax.ShapeDtypeStruct((t, kv_rank), F32)),
        out_specs=(_row_spec(tm, q_rank), _row_spec(tm, kv_rank)),
        extras=(q_a_norm.reshape(1, -1), kv_a_norm.reshape(1, -1)),
        extra_specs=(pl.BlockSpec((1, q_rank), lambda i, j: (0, 0)),
                     pl.BlockSpec((1, kv_rank), lambda i, j: (0, 0))),
        scratch_shapes=(pltpu.VMEM((nq + 1, tm, tn), F32),), w_is_nk=True)


def _rope_key(h, w_in, col_block0, cosf, sinf, rope_dim, tm):
    t = h.shape[0]

    def epilogue(acc, extras, outs, scratch):
        cos_ref, sin_ref = extras
        kr_ref, krd_ref = outs
        r = _rope128(acc, cos_ref[...], sin_ref[...])
        kr_ref[...] = r[:, :rope_dim]
        lane = lax.broadcasted_iota(jnp.int32, r.shape, 1)
        krd_ref[...] = jnp.where(lane < rope_dim, r, pltpu.roll(r, rope_dim, 1)).astype(BF)

    return _mm(
        "rope_key", h, w_in, col_block0, 1, tm, LANES, epilogue,
        out_shape=(jax.ShapeDtypeStruct((t, rope_dim), F32), jax.ShapeDtypeStruct((t, LANES), BF)),
        out_specs=(_row_spec(tm, rope_dim), _row_spec(tm, LANES)),
        extras=(cosf, sinf), extra_specs=(_row_spec(tm, LANES), _row_spec(tm, LANES)), w_is_nk=True)


def _plain_proj(a, w, col_block0, n_tiles, tm, tn, out_dtype):
    def epilogue(acc, extras, outs, scratch):
        outs[0][...] = acc.astype(out_dtype)

    return _mm("plain_proj", a, w, col_block0, n_tiles, tm, tn, epilogue,
               out_shape=jax.ShapeDtypeStruct((a.shape[0], n_tiles * tn), out_dtype),
               out_specs=_tile_spec(tm, tn), w_is_nk=True)


def _gates(h, w, col_block0, b_gate, tm, tn):
    n = b_gate.shape[0]

    def epilogue(acc, extras, outs, scratch):
        outs[0][...] = jax.nn.sigmoid(acc + extras[0][...])

    return _mm("gates", h, w, col_block0, n // tn, tm, tn, epilogue,
               out_shape=jax.ShapeDtypeStruct((h.shape[0], n), F32),
               out_specs=_tile_spec(tm, tn),
               extras=(b_gate.reshape(1, n),), extra_specs=(_col_spec(tn),), w_is_nk=True)


def _out_proj(u, w_out, x, tm, tn):
    def epilogue(acc, extras, outs, scratch):
        outs[0][...] = extras[0][...] + acc

    t, d = x.shape
    return _mm("out_proj", u, w_out, 0, d // tn, tm, tn, epilogue,
               out_shape=jax.ShapeDtypeStruct((t, d), F32), out_specs=_tile_spec(tm, tn),
               extras=(x,), extra_specs=(_tile_spec(tm, tn),))


def _q_kernel(a_ref, wn_ref, wr_ref, wk_ref, cos_ref, sin_ref, ql_ref, qr_ref):
    a = a_ref[...]
    for s in range(2):
        qn = _dot(a, wn_ref[s].astype(BF)).astype(BF)
        ql_ref[s] = _dot(qn, wk_ref[s].astype(BF)).astype(BF)
    r = _rope128(_dot(a, wr_ref[...].astype(BF)), cos_ref[...], sin_ref[...])
    lane = lax.broadcasted_iota(jnp.int32, r.shape, 1)
    qr_ref[0] = jnp.where(lane < 64, r, 0.0).astype(BF)
    qr_ref[1] = jnp.where(lane >= 64, r, 0.0).astype(BF)


def _mla_queries(cq, w_nope, w_rope_pair, w_kt, cosf, sinf, tm):
    t, qr = cq.shape
    heads, _, nope = w_nope.shape
    kv_rank = w_kt.shape[2]
    return pl.pallas_call(
        _q_kernel,
        grid=(t // tm, heads // 2),
        in_specs=[pl.BlockSpec((tm, qr), lambda i, p: (i, 0)),
                  pl.BlockSpec((2, qr, nope), lambda i, p: (p, 0, 0)),
                  pl.BlockSpec((None, qr, LANES), lambda i, p: (p, 0, 0)),
                  pl.BlockSpec((2, nope, kv_rank), lambda i, p: (p, 0, 0)),
                  pl.BlockSpec((tm, LANES), lambda i, p: (i, 0)),
                  pl.BlockSpec((tm, LANES), lambda i, p: (i, 0))],
        out_specs=(pl.BlockSpec((2, tm, kv_rank), lambda i, p: (p, i, 0)),
                   pl.BlockSpec((2, tm, LANES), lambda i, p: (p, i, 0))),
        out_shape=(jax.ShapeDtypeStruct((heads, t, kv_rank), BF),
                   jax.ShapeDtypeStruct((heads, t, LANES), BF)),
        compiler_params=_cp("parallel", "arbitrary"), name="mla_queries",
    )(cq, w_nope, w_rope_pair, w_kt, cosf, sinf)


def _vup_kernel(o_ref, w_ref, out_ref):
    out_ref[...] = _dot(o_ref[...], w_ref[...].astype(BF)).astype(out_ref.dtype)


def _value_up(o_lat, w_v, tm):
    heads, t, r = o_lat.shape
    v_dim = w_v.shape[2]
    return pl.pallas_call(
        _vup_kernel,
        grid=(t // tm, heads),
        in_specs=[pl.BlockSpec((None, tm, r), lambda i, h: (h, i, 0)),
                  pl.BlockSpec((None, r, v_dim), lambda i, h: (h, 0, 0))],
        out_specs=pl.BlockSpec((tm, v_dim), lambda i, h: (i, h)),
        out_shape=jax.ShapeDtypeStruct((t, heads * v_dim), BF),
        compiler_params=_cp("parallel", "arbitrary"), name="value_up",
    )(o_lat, w_v)


def _mla_prompt_kernel(q_ref, qr_ref, kv_ref, kr_ref, o_ref, m_ref, l_ref, acc_ref, *, tq, tk, scale):
    i = pl.program_id(1)
    j = pl.program_id(2)
    heads = q_ref.shape[0]
    rows = heads * tq
    j_last = (i * tq + tq - 1) // tk

    @pl.when(j == 0)
    def _():
        m_ref[...] = jnp.full(m_ref.shape, NEG_INF, F32)
        l_ref[...] = jnp.zeros(l_ref.shape, F32)
        acc_ref[...] = jnp.zeros(acc_ref.shape, F32)

    @pl.when(j <= j_last)
    def _():
        q = q_ref[...].reshape(rows, q_ref.shape[2])
        qr = qr_ref[...].reshape(rows, qr_ref.shape[2])
        kv = kv_ref[...].astype(BF)
        s = (_dot_nt(q, kv) + _dot_nt(qr, kr_ref[...])) * scale
        q_pos = i * tq + (lax.broadcasted_iota(jnp.int32, s.shape, 0) & (tq - 1))
        k_pos = j * tk + lax.broadcasted_iota(jnp.int32, s.shape, 1)
        s = jnp.where(k_pos <= q_pos, s, NEG_INF)
        m_prev = m_ref[...]
        m_new = jnp.maximum(m_prev, jnp.max(s, axis=-1, keepdims=True))
        alpha = jnp.exp(m_prev - m_new)
        p = jnp.exp(s - m_new)
        l_ref[...] = alpha * l_ref[...] + jnp.sum(p, axis=-1, keepdims=True)
        acc_ref[...] = alpha * acc_ref[...] + _dot(p.astype(BF), kv)
        m_ref[...] = m_new

    @pl.when(j == j_last)
    def _():
        o = acc_ref[...] * (1.0 / l_ref[...])
        o_ref[...] = o.reshape(o_ref.shape).astype(o_ref.dtype)


def _mla_prompt(q_lat, q_rope, ckv, krd, batch, seq, tq, tk, scale):
    heads, _, r = q_lat.shape
    nq, nk = seq // tq, seq // tk
    assert tq & (tq - 1) == 0

    def kmap(b, i, j):
        return (b * nk + jnp.minimum(j, (i * tq + tq - 1) // tk), 0)

    return pl.pallas_call(
        functools.partial(_mla_prompt_kernel, tq=tq, tk=tk, scale=scale),
        grid=(batch, nq, nk),
        in_specs=[pl.BlockSpec((heads, tq, r), lambda b, i, j: (0, b * nq + i, 0)),
                  pl.BlockSpec((heads, tq, LANES), lambda b, i, j: (0, b * nq + i, 0)),
                  pl.BlockSpec((tk, r), kmap),
                  pl.BlockSpec((tk, LANES), kmap)],
        out_specs=pl.BlockSpec((heads, tq, r), lambda b, i, j: (0, b * nq + i, 0)),
        out_shape=jax.ShapeDtypeStruct((heads, batch * seq, r), BF),
        scratch_shapes=[pltpu.VMEM((heads * tq, 1), F32), pltpu.VMEM((heads * tq, 1), F32),
                        pltpu.VMEM((heads * tq, r), F32)],
        compiler_params=_cp("parallel", "parallel", "arbitrary"), name="mla_prompt",
    )(q_lat, q_rope, ckv, krd)


def _strict_upper(n):
    return (lax.broadcasted_iota(jnp.int32, (n, n), 0) > lax.broadcasted_iota(jnp.int32, (n, n), 1)).astype(BF)


def _sb_block(z, mask, carry, v):
    sp = jnp.log(1.0 + jnp.exp(-jnp.abs(z)))
    log_beta = jnp.minimum(z, 0.0) - sp
    log_keep = -jnp.maximum(z, 0.0) - sp
    if mask is not None:
        log_keep = jnp.where(mask, log_keep, 0.0)
    upper = _strict_upper(z.shape[1])
    hi = log_keep.astype(BF)
    lo = (log_keep - hi.astype(F32)).astype(BF)
    after = _dot(hi, upper) + _dot(lo, upper) + carry
    a = jnp.exp(log_beta + after)
    if mask is not None:
        a = jnp.where(mask, a, 0.0)
    return _dot(a.astype(BF), v), carry + jnp.sum(log_keep, axis=-1, keepdims=True)


def _sb_unmasked_blocks(z, carry, v, blk):
    n = z.shape[1]
    sp = jnp.log(1.0 + jnp.exp(-jnp.abs(z)))
    log_beta = jnp.minimum(z, 0.0) - sp
    log_keep = -jnp.maximum(z, 0.0) - sp
    upper = _strict_upper(blk)
    cols = [slice(b * blk, (b + 1) * blk) for b in range(n // blk)]
    sums = [jnp.sum(log_keep[:, c], axis=-1, keepdims=True) for c in cols]
    later = carry
    mass_after = [None] * len(cols)
    for b in reversed(range(len(cols))):
        mass_after[b] = later
        later = later + sums[b]
    parts = []
    for b, c in enumerate(cols):
        hi = log_keep[:, c].astype(BF)
        lo = (log_keep[:, c] - hi.astype(F32)).astype(BF)
        after = _dot(hi, upper) + _dot(lo, upper) + mass_after[b]
        parts.append(jnp.exp(log_beta[:, c] + after).astype(BF))
    return _dot(jnp.concatenate(parts, axis=1), v), later


def _sb_prompt_kernel(q_ref, k_ref, v_ref, o_ref, qrows_ref, carry_ref, acc_ref, *, tq, tk, group, scale):
    i = pl.program_id(2)
    jj = pl.program_id(3)
    dh = k_ref.shape[1]
    j_max = (i * tq + tq - 1) // tk

    @pl.when(jj == 0)
    def _():
        for g in range(group):
            qrows_ref[g * tq:(g + 1) * tq, :] = q_ref[:, g * dh:(g + 1) * dh]
        carry_ref[...] = jnp.zeros(carry_ref.shape, F32)
        acc_ref[...] = jnp.zeros(acc_ref.shape, F32)

    @pl.when(jj <= j_max)
    def _():
        j = j_max - jj
        z = _dot_nt(qrows_ref[...], k_ref[...].astype(BF)) * scale
        q_pos = i * tq + (lax.broadcasted_iota(jnp.int32, z.shape, 0) & (tq - 1))
        k_pos = j * tk + lax.broadcasted_iota(jnp.int32, z.shape, 1)
        o, carry = _sb_block(z, k_pos < q_pos, carry_ref[...], v_ref[...].astype(BF))
        acc_ref[...] += o
        carry_ref[...] = carry

    @pl.when(jj == j_max)
    def _():
        for g in range(group):
            o_ref[:, g * dh:(g + 1) * dh] = acc_ref[g * tq:(g + 1) * tq, :].astype(o_ref.dtype)


def _sb_prompt(sb_q, sb_k, sb_v, batch, seq, kv_heads, tq, tk, scale):
    dh = sb_k.shape[1] // kv_heads
    group = sb_q.shape[1] // (kv_heads * dh)
    nq, nk = seq // tq, seq // tk
    assert tq & (tq - 1) == 0

    def kmap(b, c, i, jj):
        return (b * nk + jnp.maximum((i * tq + tq - 1) // tk - jj, 0), c)

    return pl.pallas_call(
        functools.partial(_sb_prompt_kernel, tq=tq, tk=tk, group=group, scale=scale),
        grid=(batch, kv_heads, nq, nk),
        in_specs=[pl.BlockSpec((tq, group * dh), lambda b, c, i, jj: (b * nq + i, c)),
                  pl.BlockSpec((tk, dh), kmap),
                  pl.BlockSpec((tk, dh), kmap)],
        out_specs=pl.BlockSpec((tq, group * dh), lambda b, c, i, jj: (b * nq + i, c)),
        out_shape=jax.ShapeDtypeStruct((batch * seq, sb_q.shape[1]), BF),
        scratch_shapes=[pltpu.VMEM((group * tq, dh), BF), pltpu.VMEM((group * tq, 1), F32),
                        pltpu.VMEM((group * tq, dh), F32)],
        compiler_params=_cp("parallel", "parallel", "parallel", "arbitrary"), name="sb_prompt",
    )(sb_q, sb_k, sb_v)


def _decode_kernel(pt_ref, q_ref, qr_ref, qbd_ref, *rest, n_pg, dec_seq, kv_heads, mla_scale, sb_scale):
    kv_refs = rest[:n_pg]
    kr_refs = rest[n_pg:2 * n_pg]
    k_refs = rest[2 * n_pg:3 * n_pg]
    v_refs = rest[3 * n_pg:4 * n_pg]
    (nkv_ref, nkr_ref, nk_ref, nv_ref, o_ref, osb_ref,
     m_ref, l_ref, acc_ref, carry_ref, sacc_ref) = rest[4 * n_pg:]
    g = pl.program_id(1)
    q = q_ref[...]
    qr = qr_ref[...]
    qbd = qbd_ref[...]
    page = nk_ref.shape[0] // kv_heads

    def heads_on_lanes(ref):
        return jnp.concatenate([ref[pl.ds(c, page, stride=kv_heads), :] for c in range(kv_heads)], axis=1).astype(BF)

    def sb_step(k, v, mask):
        z = _dot_nt(qbd, k) * sb_scale
        o, carry = _sb_block(z, mask, carry_ref[...], v)
        sacc_ref[...] += o
        carry_ref[...] = carry

    @pl.when(g == 0)
    def _():
        m_ref[...] = jnp.full(m_ref.shape, NEG_INF, F32)
        l_ref[...] = jnp.zeros(l_ref.shape, F32)
        acc_ref[...] = jnp.zeros(acc_ref.shape, F32)
        carry_ref[...] = jnp.zeros(carry_ref.shape, F32)
        sacc_ref[...] = jnp.zeros(sacc_ref.shape, F32)
        shape = (qbd.shape[0], page)
        row = lax.broadcasted_iota(jnp.int32, shape, 0)
        col = lax.broadcasted_iota(jnp.int32, shape, 1)
        sb_step(heads_on_lanes(nk_ref), heads_on_lanes(nv_ref), col < (row & (dec_seq - 1)))

    def update(s, kvs):
        m_prev = m_ref[...]
        m_new = jnp.maximum(m_prev, jnp.max(s, axis=-1, keepdims=True))
        alpha = jnp.exp(m_prev - m_new)
        p = jnp.exp(s - m_new)
        l_ref[...] = alpha * l_ref[...] + jnp.sum(p, axis=-1, keepdims=True)
        p = p.astype(BF)
        n = kvs[0].shape[0]
        pv = sum(_dot(p[:, t * n:(t + 1) * n], kvs[t]) for t in range(len(kvs)))
        acc_ref[...] = alpha * acc_ref[...] + pv
        m_ref[...] = m_new

    kvs = [r[...].astype(BF) for r in kv_refs]
    s = jnp.concatenate([_dot_nt(q, kvs[t]) + _dot(qr, kr_refs[t][...].astype(BF)) for t in range(n_pg)],
                        axis=1) * mla_scale
    update(s, kvs)

    k = jnp.concatenate([heads_on_lanes(k_refs[t]) for t in reversed(range(n_pg))], axis=0)
    v = jnp.concatenate([heads_on_lanes(v_refs[t]) for t in reversed(range(n_pg))], axis=0)
    o, carry = _sb_unmasked_blocks(_dot_nt(qbd, k) * sb_scale, carry_ref[...], v, 2 * page)
    sacc_ref[...] += o
    carry_ref[...] = carry

    @pl.when(g == pl.num_programs(1) - 1)
    def _():
        nkv = nkv_ref[...].astype(BF)
        sn = (_dot_nt(q, nkv) + _dot(qr, nkr_ref[...].astype(BF))) * mla_scale
        row = lax.broadcasted_iota(jnp.int32, sn.shape, 0)
        col = lax.broadcasted_iota(jnp.int32, sn.shape, 1)
        update(jnp.where(col <= (row & (dec_seq - 1)), sn, NEG_INF), [nkv])
        o_ref[...] = (acc_ref[...] * (1.0 / l_ref[...])).astype(o_ref.dtype)
        rows = sacc_ref.shape[0] // kv_heads
        dh = sacc_ref.shape[1] // kv_heads
        for c in range(kv_heads):
            osb_ref[c] = sacc_ref[c * rows:(c + 1) * rows, c * dh:(c + 1) * dh].astype(osb_ref.dtype)


def _decode(page_table, q_s, qr_s, q_bd, cache_kv, cache_kr_t, cache_k, cache_v,
            new_kv, new_kr_t, new_k, new_v, layer, dec_seq, kv_heads, mla_scale, sb_scale):
    db, rows, r = q_s.shape
    rows2, width = q_bd.shape[1:]
    dh = width // kv_heads
    n_pages = page_table.shape[1]
    page = cache_kv.shape[2]
    rope = cache_kr_t.shape[2]
    prow = cache_k.shape[2]
    n_pg = min(DECODE_PAGES, n_pages)
    assert n_pages % n_pg == 0 and n_pg % 2 == 0 and dec_seq & (dec_seq - 1) == 0

    def fwd_page(shape, t):
        return pl.BlockSpec((None, None) + shape, lambda b, g, pt: (layer, pt[b * n_pages + g * n_pg + t], 0, 0))

    def rev_page(t):
        return pl.BlockSpec((None, None, prow, dh),
                            lambda b, g, pt: (layer, pt[b * n_pages + n_pages - 1 - (g * n_pg + t)], 0, 0))

    def per_seq(shape):
        return pl.BlockSpec((None,) + shape, lambda b, g, pt: (b,) + (0,) * len(shape))

    in_specs = ([per_seq((rows, r)), per_seq((rows, rope)), per_seq((rows2, width))]
                + [fwd_page((page, r), t) for t in range(n_pg)]
                + [fwd_page((rope, page), t) for t in range(n_pg)]
                + [rev_page(t) for t in range(n_pg)] + [rev_page(t) for t in range(n_pg)]
                + [per_seq((page, r)), per_seq((rope, page)), per_seq((prow, dh)), per_seq((prow, dh))])
    return pl.pallas_call(
        functools.partial(_decode_kernel, n_pg=n_pg, dec_seq=dec_seq, kv_heads=kv_heads,
                          mla_scale=mla_scale, sb_scale=sb_scale),
        grid_spec=pltpu.PrefetchScalarGridSpec(
            num_scalar_prefetch=1,
            grid=(db, n_pages // n_pg),
            in_specs=in_specs,
            out_specs=(per_seq((rows, r)), per_seq((kv_heads, rows2 // kv_heads, dh))),
            scratch_shapes=[pltpu.VMEM((rows, 1), F32), pltpu.VMEM((rows, 1), F32), pltpu.VMEM((rows, r), F32),
                            pltpu.VMEM((rows2, 1), F32), pltpu.VMEM((rows2, width), F32)]),
        out_shape=(jax.ShapeDtypeStruct((db, rows, r), BF),
                   jax.ShapeDtypeStruct((db, kv_heads, rows2 // kv_heads, dh), BF)),
        compiler_params=_cp("parallel", "arbitrary"), name="decode",
    )(page_table.reshape(-1), q_s, qr_s, q_bd, *([cache_kv] * n_pg), *([cache_kr_t] * n_pg),
      *([cache_k] * n_pg), *([cache_v] * n_pg), new_kv, new_kr_t, new_k, new_v)


def _merge_kernel(vm_ref, os_ref, wm_ref, ws_ref, ga_ref, gb_ref, u_ref):
    ya = _dot(vm_ref[...], wm_ref[...].astype(BF))
    yb = _dot(os_ref[...], ws_ref[...].astype(BF))
    u_ref[...] = (ga_ref[...] * ya + gb_ref[...] * yb).astype(u_ref.dtype)


def _merge(v_mla, o_sb, w_branch_mla, w_branch_sb, gates, tm, tn):
    t, ka = v_mla.shape
    kb = o_sb.shape[1]
    d = w_branch_mla.shape[1]
    nj = d // tn
    return pl.pallas_call(
        _merge_kernel,
        grid=(t // tm, nj),
        in_specs=[pl.BlockSpec((tm, ka), lambda i, j: (i, 0)),
                  pl.BlockSpec((tm, kb), lambda i, j: (i, 0)),
                  pl.BlockSpec((ka, tn), lambda i, j: (0, j)),
                  pl.BlockSpec((kb, tn), lambda i, j: (0, j)),
                  pl.BlockSpec((tm, tn), lambda i, j: (i, j)),
                  pl.BlockSpec((tm, tn), lambda i, j: (i, nj + j))],
        out_specs=pl.BlockSpec((tm, tn), lambda i, j: (i, j)),
        out_shape=jax.ShapeDtypeStruct((t, d), BF),
        compiler_params=_cp("parallel", "arbitrary"), name="merge",
    )(v_mla, o_sb, w_branch_mla, w_branch_sb, gates, gates)


def _router_kernel(x_ref, g_ref, w_ref, b_ref, idx_ref, gate_ref, h_ref):
    h = _rms(x_ref[...]) * g_ref[...]
    h_ref[...] = h
    logits = jnp.dot(h, w_ref[...], preferred_element_type=F32, precision=lax.Precision.HIGHEST) + b_ref[...]
    n_exp = logits.shape[1]
    col = lax.broadcasted_iota(jnp.int32, logits.shape, 1)
    slot = lax.broadcasted_iota(jnp.int32, idx_ref.shape, 1)
    idx_out = jnp.zeros(idx_ref.shape, jnp.int32)
    val_out = jnp.zeros(gate_ref.shape, F32)
    top = None
    for k in range(TOP_K):
        m = jnp.max(logits, axis=-1, keepdims=True)
        idx = jnp.min(jnp.where(logits == m, col, n_exp), axis=-1, keepdims=True)
        top = m if top is None else top
        idx_out = jnp.where(slot == k, idx, idx_out)
        val_out = jnp.where(slot == k, jnp.exp(m - top), val_out)
        logits = jnp.where(col == idx, -jnp.inf, logits)
    idx_ref[...] = idx_out
    gate_ref[...] = val_out / jnp.sum(val_out, axis=-1, keepdims=True)


def _router(x, g, w_router, b_router, tm):
    t, d = x.shape
    n_exp = w_router.shape[1]
    return pl.pallas_call(
        _router_kernel,
        grid=(t // tm,),
        in_specs=[pl.BlockSpec((tm, d), lambda i: (i, 0)),
                  pl.BlockSpec((1, d), lambda i: (0, 0)),
                  pl.BlockSpec((d, n_exp), lambda i: (0, 0)),
                  pl.BlockSpec((1, n_exp), lambda i: (0, 0))],
        out_specs=(pl.BlockSpec((tm, TOP_K), lambda i: (i, 0)), pl.BlockSpec((tm, TOP_K), lambda i: (i, 0)),
                   pl.BlockSpec((tm, d), lambda i: (i, 0))),
        out_shape=(jax.ShapeDtypeStruct((t, TOP_K), jnp.int32), jax.ShapeDtypeStruct((t, TOP_K), F32),
                   jax.ShapeDtypeStruct((t, d), F32)),
        compiler_params=_cp("parallel"), name="router",
    )(x, g.reshape(1, d), w_router, b_router.reshape(1, n_exp))


ROW_CHUNK = 16


def _row_copy(src_hbm, row, dst, dst_row, sem):
    return pltpu.make_async_copy(src_hbm.at[pl.ds(row, 1), :], dst.at[pl.ds(dst_row, 1), :], sem)


def _start_rows(src_hbm, idx_ref, idx0, idx_stride, n, dst, dst_row0, sem):
    def issue(r, c):
        _row_copy(src_hbm, idx_ref[idx0 + r * idx_stride], dst, dst_row0 + r, sem).start()
        return c

    lax.fori_loop(0, n, issue, 0, unroll=8)


def _wait_rows(src_hbm, dst, sem):
    pltpu.make_async_copy(src_hbm.at[pl.ds(0, dst.shape[0]), :], dst, sem).wait()


def _gather_rows_kernel(tok_ref, x_hbm, o_ref, buf, sem):
    tm = o_ref.shape[0]
    i = pl.program_id(0)
    slot = i % 2

    def start(block, s):
        _start_rows(x_hbm, tok_ref, block * tm, 1, tm, buf.at[s], 0, sem.at[s])

    @pl.when(i == 0)
    def _():
        start(0, 0)

    @pl.when(i + 1 < pl.num_programs(0))
    def _():
        start(i + 1, 1 - slot)

    _wait_rows(x_hbm, buf.at[slot], sem.at[slot])

    def chunk(c, carry):
        rows = pl.ds(pl.multiple_of(c * ROW_CHUNK, ROW_CHUNK), ROW_CHUNK)
        o_ref[rows, :] = buf[slot, rows, :].astype(o_ref.dtype)
        return carry

    lax.fori_loop(0, tm // ROW_CHUNK, chunk, 0, unroll=2)


def _gather_rows(x, row_tok, tm):
    t, d = x.shape
    n_rows = row_tok.shape[0]
    return pl.pallas_call(
        _gather_rows_kernel,
        grid_spec=pltpu.PrefetchScalarGridSpec(
            num_scalar_prefetch=1,
            grid=(n_rows // tm,),
            in_specs=[pl.BlockSpec(memory_space=pl.ANY)],
            out_specs=pl.BlockSpec((tm, d), lambda i, tok: (i, 0)),
            scratch_shapes=[pltpu.VMEM((2, tm, d), F32), pltpu.SemaphoreType.DMA((2,))]),
        out_shape=jax.ShapeDtypeStruct((n_rows, d), BF),
        compiler_params=_cp("arbitrary"), name="gather_rows",
    )(row_tok, x)


def _expert_rows(first_ref, count_ref, x_hbm, o_hbm, xbuf, obuf, sem_in, sem_out, compute):
    j, e = pl.program_id(0), pl.program_id(1)
    n = count_ref[e]
    row0 = first_ref[e]
    tm, tn = obuf.shape[1:]
    col0 = pl.multiple_of(j * tn, tn)

    def rows(b):
        return pl.ds(pl.multiple_of((row0 + b) * tm, tm), tm)

    def x_copy(b, slot):
        return pltpu.make_async_copy(x_hbm.at[rows(b), :], xbuf.at[slot], sem_in.at[slot])

    def o_copy(b, slot):
        return pltpu.make_async_copy(obuf.at[slot], o_hbm.at[rows(b), pl.ds(col0, tn)], sem_out.at[slot])

    @pl.when(n > 0)
    def _():
        x_copy(0, 0).start()

        def body(b, carry):
            slot = b % 2

            @pl.when(b + 1 < n)
            def _():
                x_copy(b + 1, 1 - slot).start()

            x_copy(b, slot).wait()

            @pl.when(b >= 2)
            def _():
                o_copy(b - 2, slot).wait()

            obuf[slot] = compute(xbuf[slot])
            o_copy(b, slot).start()
            return carry

        lax.fori_loop(0, n, body, 0)

        @pl.when(n >= 2)
        def _():
            o_copy(n - 2, n % 2).wait()

        o_copy(n - 1, (n - 1) % 2).wait()


def _expert_up_kernel(first_ref, count_ref, x_hbm, wg_ref, wu_ref, bg_ref, bu_ref, h_hbm, xbuf, hbuf, sem_in, sem_out):
    def compute(x):
        g = _dot(x, wg_ref[...].astype(BF)) + bg_ref[...]
        u = _dot(x, wu_ref[...].astype(BF)) + bu_ref[...]
        g = jnp.minimum(g, SWIGLU_LIMIT)
        u = jnp.clip(u, -SWIGLU_LIMIT, SWIGLU_LIMIT)
        return (g * jax.nn.sigmoid(SWIGLU_ALPHA * g) * (u + 1.0)).astype(hbuf.dtype)

    _expert_rows(first_ref, count_ref, x_hbm, h_hbm, xbuf, hbuf, sem_in, sem_out, compute)


def _expert_up(xs, first_block, n_blocks, w_gate_up, b_gate_up, layer, tm, tn):
    n_rows, d = xs.shape
    d_ff = w_gate_up.shape[3] // 2
    nj = d_ff // tn
    n_exp = w_gate_up.shape[1]
    b3 = b_gate_up.reshape(b_gate_up.shape[0], n_exp, 1, 2 * d_ff)
    return pl.pallas_call(
        _expert_up_kernel,
        grid_spec=pltpu.PrefetchScalarGridSpec(
            num_scalar_prefetch=2,
            grid=(nj, n_exp),
            in_specs=[pl.BlockSpec(memory_space=pl.ANY),
                      pl.BlockSpec((None, None, d, tn), lambda j, e, fb, nb: (layer, e, 0, j)),
                      pl.BlockSpec((None, None, d, tn), lambda j, e, fb, nb: (layer, e, 0, nj + j)),
                      pl.BlockSpec((None, None, 1, tn), lambda j, e, fb, nb: (layer, e, 0, j)),
                      pl.BlockSpec((None, None, 1, tn), lambda j, e, fb, nb: (layer, e, 0, nj + j))],
            out_specs=pl.BlockSpec(memory_space=pl.ANY),
            scratch_shapes=[pltpu.VMEM((2, tm, d), BF), pltpu.VMEM((2, tm, tn), BF),
                            pltpu.SemaphoreType.DMA((2,)), pltpu.SemaphoreType.DMA((2,))]),
        out_shape=jax.ShapeDtypeStruct((n_rows, d_ff), BF),
        compiler_params=_cp("arbitrary", "arbitrary"), name="expert_up",
    )(first_block, n_blocks, xs, w_gate_up, w_gate_up, b3, b3)


def _expert_down_kernel(first_ref, count_ref, h_hbm, w_ref, b_ref, y_hbm, hbuf, ybuf, sem_in, sem_out):
    def compute(h):
        return _dot(h, w_ref[...].astype(BF)) + b_ref[...]

    _expert_rows(first_ref, count_ref, h_hbm, y_hbm, hbuf, ybuf, sem_in, sem_out, compute)


def _expert_down(hs, first_block, n_blocks, w_down, b_down, layer, tm, tn):
    n_rows, d_ff = hs.shape
    n_exp, _, d = w_down.shape[1:]
    b3 = b_down.reshape(b_down.shape[0], n_exp, 1, d)
    return pl.pallas_call(
        _expert_down_kernel,
        grid_spec=pltpu.PrefetchScalarGridSpec(
            num_scalar_prefetch=2,
            grid=(d // tn, n_exp),
            in_specs=[pl.BlockSpec(memory_space=pl.ANY),
                      pl.BlockSpec((None, None, d_ff, tn), lambda j, e, fb, nb: (layer, e, 0, j)),
                      pl.BlockSpec((None, None, 1, tn), lambda j, e, fb, nb: (layer, e, 0, j))],
            out_specs=pl.BlockSpec(memory_space=pl.ANY),
            scratch_shapes=[pltpu.VMEM((2, tm, d_ff), BF), pltpu.VMEM((2, tm, tn), F32),
                            pltpu.SemaphoreType.DMA((2,)), pltpu.SemaphoreType.DMA((2,))]),
        out_shape=jax.ShapeDtypeStruct((n_rows, d), F32),
        compiler_params=_cp("arbitrary", "arbitrary"), name="expert_down",
    )(first_block, n_blocks, hs, w_down, b3)


def _combine_kernel(dest_ref, y_hbm, gate_ref, x_ref, g_ref, op_ref, os_ref, buf, sem, *, n_first):
    tc = x_ref.shape[0]
    i = pl.program_id(0)
    slot = i % 2

    def start(block, s):
        for k in range(TOP_K):
            _start_rows(y_hbm, dest_ref, block * tc * TOP_K + k, TOP_K, tc, buf.at[s], k * tc, sem.at[s])

    @pl.when(i == 0)
    def _():
        start(0, 0)

    @pl.when(i + 1 < pl.num_programs(0))
    def _():
        start(i + 1, 1 - slot)

    _wait_rows(y_hbm, buf.at[slot], sem.at[slot])

    def write(o_ref):
        def chunk(c, carry):
            rows = pl.ds(pl.multiple_of(c * 8, 8), 8)
            gate = gate_ref[rows, :]
            moe = sum(buf[slot, pl.ds(pl.multiple_of(k * tc + c * 8, 8), 8), :] * gate[:, k:k + 1]
                      for k in range(TOP_K))
            o_ref[rows, :] = _rms(x_ref[rows, :] + moe) * g_ref[...]
            return carry

        lax.fori_loop(0, tc // 8, chunk, 0, unroll=2)

    @pl.when(i < n_first)
    def _():
        write(op_ref)

    @pl.when(i >= n_first)
    def _():
        write(os_ref)


def _combine(y_rows, dest, gate, x, g_final, tc, t_first):
    t, d = x.shape
    n_first = t_first // tc
    assert t_first % tc == 0 and 0 < n_first < t // tc
    return pl.pallas_call(
        functools.partial(_combine_kernel, n_first=n_first),
        grid_spec=pltpu.PrefetchScalarGridSpec(
            num_scalar_prefetch=1,
            grid=(t // tc,),
            in_specs=[pl.BlockSpec(memory_space=pl.ANY),
                      pl.BlockSpec((tc, TOP_K), lambda i, dst: (i, 0)),
                      pl.BlockSpec((tc, d), lambda i, dst: (i, 0)),
                      pl.BlockSpec((1, d), lambda i, dst: (0, 0))],
            out_specs=(pl.BlockSpec((tc, d), lambda i, dst: (jnp.minimum(i, n_first - 1), 0)),
                       pl.BlockSpec((tc, d), lambda i, dst: (jnp.maximum(i - n_first, 0), 0))),
            scratch_shapes=[pltpu.VMEM((2, TOP_K * tc, d), F32), pltpu.SemaphoreType.DMA((2,))]),
        out_shape=(jax.ShapeDtypeStruct((t_first, d), F32), jax.ShapeDtypeStruct((t - t_first, d), F32)),
        compiler_params=_cp("arbitrary"), name="combine",
    )(dest, y_rows, gate, x, g_final.reshape(1, d))


def _routing_tables(top_idx, n_exp, tm):
    n_pairs = top_idx.size
    flat_e = top_idx.reshape(-1)
    onehot = (flat_e[:, None] == jnp.arange(n_exp, dtype=jnp.int32)[None, :]).astype(jnp.int32)
    csum = jnp.cumsum(onehot, axis=0)
    counts = csum[-1]
    padded = (counts + tm - 1) // tm * tm
    pad_ends = jnp.cumsum(padded)
    pad_starts = pad_ends - padded
    dest = jnp.sum(onehot * (pad_starts[None, :] + csum - 1), axis=1).astype(jnp.int32)
    n_rows = (n_pairs + n_exp * (tm - 1) + tm - 1) // tm * tm
    row_tok = (jnp.arange(n_rows, dtype=jnp.int32) % (n_pairs // TOP_K)).at[dest].set(
        jnp.arange(n_pairs, dtype=jnp.int32) // TOP_K)
    return dest, row_tok, (pad_starts // tm).astype(jnp.int32), (padded // tm).astype(jnp.int32)


def _pick(n, pref):
    b = min(pref, n)
    while n % b:
        b //= 2
    return b


def kernel(x_prompt, x_sample, cache_kv_latent, cache_k_rope, cache_sb_k, cache_sb_v, page_table, norm_attn, w_in, b_gate, q_a_norm, w_uq, kv_a_norm, w_ukv, w_branch_mla, w_branch_sb, w_out, norm_ffn, w_router, b_router, w_gate_up, b_gate_up, w_down, b_down, norm_final):
    batch, seq, d = x_prompt.shape
    db, dec_seq, _ = x_sample.shape
    depth = w_in.shape[0]
    q_rank = q_a_norm.shape[1]
    kv_rank = kv_a_norm.shape[1]
    rope = cache_k_rope.shape[3]
    heads = w_ukv.shape[2]
    nope = w_uq.shape[2] // heads - rope
    page = cache_kv_latent.shape[2]
    kv_heads, dh = cache_sb_k.shape[3], cache_sb_k.shape[4]
    n_pages = page_table.shape[1]
    past_len = n_pages * page
    in_cols = w_in.shape[2]
    sb_heads = (in_cols - q_rank - kv_rank - rope - 2 * d) // dh - 2 * kv_heads
    group = sb_heads // kv_heads
    n_exp = w_router.shape[2]
    mla_scale = float(nope + rope) ** -0.5
    sb_scale = float(dh) ** -0.5
    assert rope == 64 and dh == LANES and kv_rank % LANES == 0 and (q_rank + kv_rank) % LANES == 0

    tp, ts = batch * seq, db * dec_seq
    t = tp + ts
    x = jnp.concatenate([x_prompt.reshape(tp, d), x_sample.reshape(ts, d)], axis=0)

    pos = jnp.concatenate([jnp.tile(jnp.arange(seq), batch), jnp.tile(past_len + jnp.arange(dec_seq), db)])
    inv_freq = 1.0 / (ROPE_THETA ** (jnp.arange(0, rope, 2, dtype=F32) / rope))
    ang = pos.astype(F32)[:, None] * inv_freq[None, :]
    cos, sin = jnp.cos(ang), jnp.sin(ang)
    cosf = jnp.concatenate([cos, cos, cos, cos], axis=1)
    sinf = jnp.concatenate([-sin, sin, -sin, sin], axis=1)

    tm_big = _pick(t, 1024)
    tm_mid = _pick(t, 512)
    outs_p, outs_s = [], []
    for l in range(depth):
        rest0 = q_rank + kv_rank + rope
        w_in_t = jnp.swapaxes(w_in, 1, 2)[l]
        w_rest = w_in_t[rest0:]
        w_uq3 = w_uq[l].reshape(q_rank, heads, nope + rope)
        w_nope = jnp.transpose(w_uq3[:, :, :nope], (1, 0, 2))
        w_rope_pair = jnp.transpose(w_uq3[:, :, nope:].reshape(q_rank, heads // 2, 2 * rope), (1, 0, 2))
        w_kt = jnp.transpose(w_ukv[l][:, :, :nope], (1, 2, 0))
        w_v = jnp.transpose(w_ukv[l][:, :, nope:], (1, 0, 2))

        h = _rmsnorm(x, norm_attn[l], BF, _pick(t, 256))
        cq, ckv = _latents(h, w_in_t, q_a_norm[l], kv_a_norm[l], q_rank, kv_rank, tm_mid)
        kr, krd = _rope_key(h, w_in_t, (q_rank + kv_rank) // LANES, cosf, sinf, rope, tm_big)
        c0 = sb_heads * dh
        tn = _pick(d, 512)
        assert c0 % tn == 0 and (c0 + 2 * kv_heads * dh) % tn == 0 and c0 % (kv_heads * dh) == 0
        sb_q = _plain_proj(h, w_rest, 0, c0 // tn, tm_big, tn, BF)
        sb_kv = _plain_proj(h, w_rest, c0 // (kv_heads * dh), 2, tm_big, kv_heads * dh, F32)
        sb_k, sb_v = sb_kv[:, :kv_heads * dh], sb_kv[:, kv_heads * dh:]
        gates = _gates(h, w_rest, (c0 + 2 * kv_heads * dh) // tn, b_gate[l], tm_big, tn)
        q_lat, q_rope = _mla_queries(cq, w_nope, w_rope_pair, w_kt, cosf, sinf, tm_big)

        o_lat_p = _mla_prompt(q_lat, q_rope, ckv, krd, batch, seq, min(128, seq), min(512, seq), mla_scale)
        o_sb_p = _sb_prompt(sb_q, sb_k, sb_v, batch, seq, kv_heads, min(128, seq), min(256, seq), sb_scale)

        def seq_major(a):
            return jnp.transpose(a[:, tp:].reshape(a.shape[0], db, dec_seq, a.shape[2]), (1, 0, 2, 3)).reshape(
                db, a.shape[0] * dec_seq, a.shape[2])

        def new_page(a, rows_per_pos):
            a = a[tp:].reshape(db, dec_seq * rows_per_pos, a.shape[1] // rows_per_pos)
            return jnp.pad(a, ((0, 0), (0, (page - dec_seq) * rows_per_pos), (0, 0)))

        q_s = seq_major(q_lat)
        qr_m = seq_major(q_rope)
        qr_s = qr_m[:, :, :rope] + qr_m[:, :, rope:]
        sbq_s = jnp.transpose(sb_q[tp:].reshape(db, dec_seq, kv_heads, group, dh), (0, 2, 3, 1, 4)).reshape(
            db, kv_heads, group * dec_seq, dh)
        q_bd = jnp.concatenate(
            [jnp.concatenate([sbq_s[:, c] if c2 == c else jnp.zeros_like(sbq_s[:, c]) for c2 in range(kv_heads)], axis=2)
             for c in range(kv_heads)], axis=1)
        cache_k2 = cache_sb_k.reshape(depth, -1, page * kv_heads, dh)
        cache_v2 = cache_sb_v.reshape(depth, -1, page * kv_heads, dh)
        cache_kr_t = jnp.swapaxes(cache_k_rope, 2, 3)
        o_lat_s, o_sb_s = _decode(page_table, q_s, qr_s, q_bd, cache_kv_latent, cache_kr_t, cache_k2, cache_v2,
                                  new_page(ckv, 1), jnp.swapaxes(new_page(kr, 1), 1, 2),
                                  new_page(sb_k, kv_heads), new_page(sb_v, kv_heads),
                                  l, dec_seq, kv_heads, mla_scale, sb_scale)
        o_lat_s = jnp.transpose(o_lat_s.reshape(db, heads, dec_seq, kv_rank), (1, 0, 2, 3)).reshape(heads, ts, kv_rank)
        o_sb_s = jnp.transpose(o_sb_s.reshape(db, kv_heads, group, dec_seq, dh), (0, 3, 1, 2, 4)).reshape(ts, sb_heads * dh)

        v_mla = jnp.concatenate([_value_up(o_lat_p, w_v, _pick(tp, 1024)), _value_up(o_lat_s, w_v, _pick(ts, 1024))], axis=0)
        o_sb = jnp.concatenate([o_sb_p, o_sb_s], axis=0)
        u = _merge(v_mla, o_sb, w_branch_mla[l], w_branch_sb[l], gates, tm_mid, tn)
        x = _out_proj(u, w_out[l], x, tm_big, tn)

        top_idx, gate, h_ffn = _router(x, norm_ffn[l], w_router[l], b_router[l], _pick(t, 256))
        dest, row_tok, first_block, n_blocks = _routing_tables(top_idx, n_exp, MOE_BLOCK)
        xs = _gather_rows(h_ffn, row_tok, MOE_BLOCK)
        hs = _expert_up(xs, first_block, n_blocks, w_gate_up, b_gate_up, l, MOE_BLOCK, _pick(w_gate_up.shape[3] // 2, 512))
        ys = _expert_down(hs, first_block, n_blocks, w_down, b_down, l, MOE_BLOCK, _pick(d, 1024))
        last = l == depth - 1
        assert last, "a deeper stack needs a combine variant without the final norm"
        y_p, y_s = _combine(ys, dest, gate, x, norm_final, _pick(ts, 64), tp)

        outs_p.append((ckv[:tp].reshape(batch, seq, kv_rank), kr[:tp].reshape(batch, seq, rope),
                       sb_k[:tp].reshape(batch, seq, kv_heads, dh), sb_v[:tp].reshape(batch, seq, kv_heads, dh)))
        outs_s.append((ckv[tp:].reshape(db, dec_seq, kv_rank), kr[tp:].reshape(db, dec_seq, rope),
                       sb_k[tp:].reshape(db, dec_seq, kv_heads, dh), sb_v[tp:].reshape(db, dec_seq, kv_heads, dh)))

    y_prompt = y_p.reshape(batch, seq, d)
    y_sample = y_s.reshape(db, dec_seq, d)
    stack = lambda rows, k: jnp.stack([r[k] for r in rows])
    return (y_prompt, y_sample) + tuple(stack(outs_p, k) for k in range(4)) + tuple(stack(outs_s, k) for k in range(4))
```

```python
import functools

import jax
import jax.numpy as jnp
from jax import lax
from jax.experimental import pallas as pl
from jax.experimental.pallas import tpu as pltpu

TOP_K = 4
SWIGLU_LIMIT = 7.0
SWIGLU_ALPHA = 1.702
EPS = 1e-6
NEG_INF = -1e30
ROPE_THETA = 10000.0

LANES = 128
VMEM_LIMIT = 56 * 1024 * 1024
MOE_BLOCK = 256
DECODE_PAGES = 16

BF = jnp.bfloat16
F32 = jnp.float32


def _cp(*sem):
    return pltpu.CompilerParams(dimension_semantics=sem, vmem_limit_bytes=VMEM_LIMIT)


def _dot(a, b):
    return jnp.dot(a, b, preferred_element_type=F32)


def _dot_nt(a, b):
    return lax.dot_general(a, b, (((1,), (1,)), ((), ())), preferred_element_type=F32)


def _rms(x):
    return x * lax.rsqrt(jnp.mean(x * x, axis=-1, keepdims=True) + EPS)


def _rope128(x, cosf, sinf):
    lane = lax.broadcasted_iota(jnp.int32, x.shape, 1)
    rot = jnp.where((lane & 63) < 32, pltpu.roll(x, 96, 1), pltpu.roll(x, 32, 1))
    return x * cosf + rot * sinf


def _rmsnorm_kernel(x_ref, g_ref, o_ref):
    o_ref[...] = (_rms(x_ref[...]) * g_ref[...]).astype(o_ref.dtype)


def _rmsnorm(x, g, out_dtype, tm):
    t, d = x.shape
    return pl.pallas_call(
        _rmsnorm_kernel,
        grid=(t // tm,),
        in_specs=[pl.BlockSpec((tm, d), lambda i: (i, 0)), pl.BlockSpec((1, d), lambda i: (0, 0))],
        out_specs=pl.BlockSpec((tm, d), lambda i: (i, 0)),
        out_shape=jax.ShapeDtypeStruct((t, d), out_dtype),
        compiler_params=_cp("parallel"), name="rmsnorm",
    )(x, g.reshape(1, d))


def _mm_kernel(*refs, n_extra, epilogue, n_scratch, w_is_nk):
    a_ref, w_ref = refs[0], refs[1]
    extras = refs[2:2 + n_extra]
    rest = refs[2 + n_extra:]
    outs = rest[:len(rest) - n_scratch]
    scratch = rest[len(rest) - n_scratch:]
    w = w_ref[...].astype(BF)
    acc = _dot_nt(a_ref[...], w) if w_is_nk else _dot(a_ref[...], w)
    epilogue(acc, extras, outs, scratch)


def _mm(name, a, w, col_block0, n_tiles, tm, tn, epilogue, out_shape, out_specs,
        extras=(), extra_specs=(), scratch_shapes=(), w_is_nk=False):
    m, k = a.shape
    if w_is_nk:
        w_spec = pl.BlockSpec((tn, k), lambda i, j: (col_block0 + j, 0))
    else:
        w_spec = pl.BlockSpec((k, tn), lambda i, j: (0, col_block0 + j))
    in_specs = [pl.BlockSpec((tm, k), lambda i, j: (i, 0)), w_spec] + list(extra_specs)
    kern = functools.partial(_mm_kernel, n_extra=len(extras), epilogue=epilogue,
                             n_scratch=len(scratch_shapes), w_is_nk=w_is_nk)
    return pl.pallas_call(
        kern,
        grid=(m // tm, n_tiles),
        in_specs=in_specs,
        out_specs=out_specs,
        out_shape=out_shape,
        scratch_shapes=list(scratch_shapes),
        compiler_params=_cp("parallel", "arbitrary"), name=name,
    )(a, w, *extras)


def _tile_spec(tm, tn, off=0):
    return pl.BlockSpec((tm, tn), lambda i, j: (i, off + j))


def _row_spec(tm, c):
    return pl.BlockSpec((tm, c), lambda i, j: (i, 0))


def _col_spec(tn, off=0):
    return pl.BlockSpec((1, tn), lambda i, j: (0, off + j))


def _latents(h, w_in, q_a_norm, kv_a_norm, q_rank, kv_rank, tm):
    t = h.shape[0]
    tn = kv_rank
    nq = q_rank // tn

    def epilogue(acc, extras, outs, scratch):
        gq_ref, gkv_ref = extras
        cq_ref, ckv_ref = outs
        buf, = scratch
        j = pl.program_id(1)
        buf[j] = acc

        @pl.when(j == nq)
        def _():
            ssq = sum(jnp.sum(buf[s] * buf[s], axis=-1, keepdims=True) for s in range(nq))
            rs = lax.rsqrt(ssq / q_rank + EPS)
            for s in range(nq):
                cq_ref[:, s * tn:(s + 1) * tn] = (buf[s] * rs * gq_ref[:, s * tn:(s + 1) * tn]).astype(BF)
            ckv_ref[...] = _rms(buf[nq]) * gkv_ref[...]

    return _mm(
        "latents", h, w_in, 0, nq + 1, tm, tn, epilogue,
        out_shape=(jax.ShapeDtypeStruct((t, q_rank), BF), jax.ShapeDtypeStruct((t, kv_rank), F32)),
        out_specs=(_row_spec(tm, q_rank), _row_spec(tm, kv_rank)),
        extras=(q_a_norm.reshape(1, -1), kv_a_norm.reshape(1, -1)),
        extra_specs=(pl.BlockSpec((1, q_rank), lambda i, j: (0, 0)),
                     pl.BlockSpec((1, kv_rank), lambda i, j: (0, 0))),
        scratch_shapes=(pltpu.VMEM((nq + 1, tm, tn), F32),), w_is_nk=True)


def _rope_key(h, w_in, col_block0, cosf, sinf, rope_dim, tm):
    t = h.shape[0]

    def epilogue(acc, extras, outs, scratch):
        cos_ref, sin_ref = extras
        kr_ref, krd_ref = outs
        r = _rope128(acc, cos_ref[...], sin_ref[...])
        kr_ref[...] = r[:, :rope_dim]
        lane = lax.broadcasted_iota(jnp.int32, r.shape, 1)
        krd_ref[...] = jnp.where(lane < rope_dim, r, pltpu.roll(r, rope_dim, 1)).astype(BF)

    return _mm(
        "rope_key", h, w_in, col_block0, 1, tm, LANES, epilogue,
        out_shape=(jax.ShapeDtypeStruct((t, rope_dim), F32), jax.ShapeDtypeStruct((t, LANES), BF)),
        out_specs=(_row_spec(tm, rope_dim), _row_spec(tm, LANES)),
        extras=(cosf, sinf), extra_specs=(_row_spec(tm, LANES), _row_spec(tm, LANES)), w_is_nk=True)


def _plain_proj(a, w, col_block0, n_tiles, tm, tn, out_dtype):
    def epilogue(acc, extras, outs, scratch):
        outs[0][...] = acc.astype(out_dtype)

    return _mm("plain_proj", a, w, col_block0, n_tiles, tm, tn, epilogue,
               out_shape=jax.ShapeDtypeStruct((a.shape[0], n_tiles * tn), out_dtype),
               out_specs=_tile_spec(tm, tn), w_is_nk=True)


def _gates(h, w, col_block0, b_gate, tm, tn):
    n = b_gate.shape[0]

    def epilogue(acc, extras, outs, scratch):
        outs[0][...] = jax.nn.sigmoid(acc + extras[0][...])

    return _mm("gates", h, w, col_block0, n // tn, tm, tn, epilogue,
               out_shape=jax.ShapeDtypeStruct((h.shape[0], n), F32),
               out_specs=_tile_spec(tm, tn),
               extras=(b_gate.reshape(1, n),), extra_specs=(_col_spec(tn),), w_is_nk=True)


def _out_proj(u, w_out, x, tm, tn):
    def epilogue(acc, extras, outs, scratch):
        outs[0][...] = extras[0][...] + acc

    t, d = x.shape
    return _mm("out_proj", u, w_out, 0, d // tn, tm, tn, epilogue,
               out_shape=jax.ShapeDtypeStruct((t, d), F32), out_specs=_tile_spec(tm, tn),
               extras=(x,), extra_specs=(_tile_spec(tm, tn),))


def _q_kernel(a_ref, wn_ref, wr_ref, wk_ref, cos_ref, sin_ref, ql_ref, qr_ref):
    a = a_ref[...]
    for s in range(2):
        qn = _dot(a, wn_ref[s].astype(BF)).astype(BF)
        ql_ref[s] = _dot(qn, wk_ref[s].astype(BF)).astype(BF)
    r = _rope128(_dot(a, wr_ref[...].astype(BF)), cos_ref[...], sin_ref[...])
    lane = lax.broadcasted_iota(jnp.int32, r.shape, 1)
    qr_ref[0] = jnp.where(lane < 64, r, 0.0).astype(BF)
    qr_ref[1] = jnp.where(lane >= 64, r, 0.0).astype(BF)


def _mla_queries(cq, w_nope, w_rope_pair, w_kt, cosf, sinf, tm):
    t, qr = cq.shape
    heads, _, nope = w_nope.shape
    kv_rank = w_kt.shape[2]
    return pl.pallas_call(
        _q_kernel,
        grid=(t // tm, heads // 2),
        in_specs=[pl.BlockSpec((tm, qr), lambda i, p: (i, 0)),
                  pl.BlockSpec((2, qr, nope), lambda i, p: (p, 0, 0)),
                  pl.BlockSpec((None, qr, LANES), lambda i, p: (p, 0, 0)),
                  pl.BlockSpec((2, nope, kv_rank), lambda i, p: (p, 0, 0)),
                  pl.BlockSpec((tm, LANES), lambda i, p: (i, 0)),
                  pl.BlockSpec((tm, LANES), lambda i, p: (i, 0))],
        out_specs=(pl.BlockSpec((2, tm, kv_rank), lambda i, p: (p, i, 0)),
                   pl.BlockSpec((2, tm, LANES), lambda i, p: (p, i, 0))),
        out_shape=(jax.ShapeDtypeStruct((heads, t, kv_rank), BF),
                   jax.ShapeDtypeStruct((heads, t, LANES), BF)),
        compiler_params=_cp("parallel", "arbitrary"), name="mla_queries",
    )(cq, w_nope, w_rope_pair, w_kt, cosf, sinf)


def _vup_kernel(o_ref, w_ref, out_ref):
    out_ref[...] = _dot(o_ref[...], w_ref[...].astype(BF)).astype(out_ref.dtype)


def _value_up(o_lat, w_v, tm):
    heads, t, r = o_lat.shape
    v_dim = w_v.shape[2]
    return pl.pallas_call(
        _vup_kernel,
        grid=(t // tm, heads),
        in_specs=[pl.BlockSpec((None, tm, r), lambda i, h: (h, i, 0)),
                  pl.BlockSpec((None, r, v_dim), lambda i, h: (h, 0, 0))],
        out_specs=pl.BlockSpec((tm, v_dim), lambda i, h: (i, h)),
        out_shape=jax.ShapeDtypeStruct((t, heads * v_dim), BF),
        compiler_params=_cp("parallel", "arbitrary"), name="value_up",
    )(o_lat, w_v)


def _mla_prompt_kernel(q_ref, qr_ref, kv_ref, kr_ref, o_ref, m_ref, l_ref, acc_ref, *, tq, tk, scale):
    i = pl.program_id(1)
    j = pl.program_id(2)
    heads = q_ref.shape[0]
    rows = heads * tq
    j_last = (i * tq + tq - 1) // tk

    @pl.when(j == 0)
    def _():
        m_ref[...] = jnp.full(m_ref.shape, NEG_INF, F32)
        l_ref[...] = jnp.zeros(l_ref.shape, F32)
        acc_ref[...] = jnp.zeros(acc_ref.shape, F32)

    def step(diagonal):
        q = q_ref[...].reshape(rows, q_ref.shape[2])
        qr = qr_ref[...].reshape(rows, qr_ref.shape[2])
        kv = kv_ref[...].astype(BF)
        s = (_dot_nt(q, kv) + _dot_nt(qr, kr_ref[...])) * scale
        if diagonal:
            q_pos = i * tq + (lax.broadcasted_iota(jnp.int32, s.shape, 0) & (tq - 1))
            k_pos = j * tk + lax.broadcasted_iota(jnp.int32, s.shape, 1)
            s = jnp.where(k_pos <= q_pos, s, NEG_INF)
        m_prev = m_ref[...]
        m_new = jnp.maximum(m_prev, jnp.max(s, axis=-1, keepdims=True))
        alpha = jnp.exp(m_prev - m_new)
        p = jnp.exp(s - m_new)
        l_ref[...] = alpha * l_ref[...] + jnp.sum(p, axis=-1, keepdims=True)
        acc_ref[...] = alpha * acc_ref[...] + _dot(p.astype(BF), kv)
        m_ref[...] = m_new

    @pl.when(j < j_last)
    def _():
        step(False)

    @pl.when(j == j_last)
    def _():
        step(True)
        o = acc_ref[...] * (1.0 / l_ref[...])
        o_ref[...] = o.reshape(o_ref.shape).astype(o_ref.dtype)


def _mla_prompt(q_lat, q_rope, ckv, krd, batch, seq, tq, tk, scale):
    heads, _, r = q_lat.shape
    nq, nk = seq // tq, seq // tk
    assert tq & (tq - 1) == 0 and tk % tq == 0

    def kmap(b, i, j):
        return (b * nk + jnp.minimum(j, (i * tq + tq - 1) // tk), 0)

    return pl.pallas_call(
        functools.partial(_mla_prompt_kernel, tq=tq, tk=tk, scale=scale),
        grid=(batch, nq, nk),
        in_specs=[pl.BlockSpec((heads, tq, r), lambda b, i, j: (0, b * nq + i, 0)),
                  pl.BlockSpec((heads, tq, LANES), lambda b, i, j: (0, b * nq + i, 0)),
                  pl.BlockSpec((tk, r), kmap),
                  pl.BlockSpec((tk, LANES), kmap)],
        out_specs=pl.BlockSpec((heads, tq, r), lambda b, i, j: (0, b * nq + i, 0)),
        out_shape=jax.ShapeDtypeStruct((heads, batch * seq, r), BF),
        scratch_shapes=[pltpu.VMEM((heads * tq, 1), F32), pltpu.VMEM((heads * tq, 1), F32),
                        pltpu.VMEM((heads * tq, r), F32)],
        compiler_params=_cp("parallel", "parallel", "arbitrary"), name="mla_prompt",
    )(q_lat, q_rope, ckv, krd)


def _strict_upper(n):
    return (lax.broadcasted_iota(jnp.int32, (n, n), 0) > lax.broadcasted_iota(jnp.int32, (n, n), 1)).astype(BF)


def _sb_block(z, mask, carry, v):
    sp = jnp.log(1.0 + jnp.exp(-jnp.abs(z)))
    log_beta = jnp.minimum(z, 0.0) - sp
    log_keep = -jnp.maximum(z, 0.0) - sp
    if mask is not None:
        log_keep = jnp.where(mask, log_keep, 0.0)
    upper = _strict_upper(z.shape[1])
    hi = log_keep.astype(BF)
    lo = (log_keep - hi.astype(F32)).astype(BF)
    after = _dot(hi, upper) + _dot(lo, upper) + carry
    a = jnp.exp(log_beta + after)
    if mask is not None:
        a = jnp.where(mask, a, 0.0)
    return _dot(a.astype(BF), v), carry + jnp.sum(log_keep, axis=-1, keepdims=True)


def _sb_unmasked_blocks(z, carry, v, blk):
    n = z.shape[1]
    sp = jnp.log(1.0 + jnp.exp(-jnp.abs(z)))
    log_beta = jnp.minimum(z, 0.0) - sp
    log_keep = -jnp.maximum(z, 0.0) - sp
    upper = _strict_upper(blk)
    cols = [slice(b * blk, (b + 1) * blk) for b in range(n // blk)]
    sums = [jnp.sum(log_keep[:, c], axis=-1, keepdims=True) for c in cols]
    later = carry
    mass_after = [None] * len(cols)
    for b in reversed(range(len(cols))):
        mass_after[b] = later
        later = later + sums[b]
    parts = []
    for b, c in enumerate(cols):
        hi = log_keep[:, c].astype(BF)
        lo = (log_keep[:, c] - hi.astype(F32)).astype(BF)
        after = _dot(hi, upper) + _dot(lo, upper) + mass_after[b]
        parts.append(jnp.exp(log_beta[:, c] + after).astype(BF))
    return _dot(jnp.concatenate(parts, axis=1), v), later


def _sb_prompt_kernel(q_ref, k_ref, v_ref, o_ref, qrows_ref, carry_ref, acc_ref, *, tq, tk, group, scale):
    i = pl.program_id(2)
    jj = pl.program_id(3)
    dh = k_ref.shape[1]
    j_max = (i * tq + tq - 1) // tk

    @pl.when(jj == 0)
    def _():
        for g in range(group):
            qrows_ref[g * tq:(g + 1) * tq, :] = q_ref[:, g * dh:(g + 1) * dh]
        carry_ref[...] = jnp.zeros(carry_ref.shape, F32)
        acc_ref[...] = jnp.zeros(acc_ref.shape, F32)

    def step(diagonal):
        z = _dot_nt(qrows_ref[...], k_ref[...].astype(BF)) * scale
        v = v_ref[...].astype(BF)
        if diagonal:
            q_pos = i * tq + (lax.broadcasted_iota(jnp.int32, z.shape, 0) & (tq - 1))
            k_pos = j_max * tk + lax.broadcasted_iota(jnp.int32, z.shape, 1)
            o, carry = _sb_block(z, k_pos < q_pos, carry_ref[...], v)
        else:
            o, carry = _sb_unmasked_blocks(z, carry_ref[...], v, tk)
        acc_ref[...] += o
        carry_ref[...] = carry

    @pl.when(jj == 0)
    def _():
        step(True)

    @pl.when((jj > 0) & (jj <= j_max))
    def _():
        step(False)

    @pl.when(jj == j_max)
    def _():
        for g in range(group):
            o_ref[:, g * dh:(g + 1) * dh] = acc_ref[g * tq:(g + 1) * tq, :].astype(o_ref.dtype)


def _sb_prompt(sb_q, sb_k, sb_v, batch, seq, kv_heads, tq, tk, scale):
    dh = sb_k.shape[1] // kv_heads
    group = sb_q.shape[1] // (kv_heads * dh)
    nq, nk = seq // tq, seq // tk
    assert tq & (tq - 1) == 0 and tk % tq == 0

    def kmap(b, c, i, jj):
        return (b * nk + jnp.maximum((i * tq + tq - 1) // tk - jj, 0), c)

    return pl.pallas_call(
        functools.partial(_sb_prompt_kernel, tq=tq, tk=tk, group=group, scale=scale),
        grid=(batch, kv_heads, nq, nk),
        in_specs=[pl.BlockSpec((tq, group * dh), lambda b, c, i, jj: (b * nq + i, c)),
                  pl.BlockSpec((tk, dh), kmap),
                  pl.BlockSpec((tk, dh), kmap)],
        out_specs=pl.BlockSpec((tq, group * dh), lambda b, c, i, jj: (b * nq + i, c)),
        out_shape=jax.ShapeDtypeStruct((batch * seq, sb_q.shape[1]), BF),
        scratch_shapes=[pltpu.VMEM((group * tq, dh), BF), pltpu.VMEM((group * tq, 1), F32),
                        pltpu.VMEM((group * tq, dh), F32)],
        compiler_params=_cp("parallel", "parallel", "parallel", "arbitrary"), name="sb_prompt",
    )(sb_q, sb_k, sb_v)


def _decode_kernel(pt_ref, q_ref, qr_ref, qbd_ref, *rest, n_pg, dec_seq, kv_heads, mla_scale, sb_scale):
    kv_refs = rest[:n_pg]
    kr_refs = rest[n_pg:2 * n_pg]
    k_refs = rest[2 * n_pg:3 * n_pg]
    v_refs = rest[3 * n_pg:4 * n_pg]
    (nkv_ref, nkr_ref, nk_ref, nv_ref, o_ref, osb_ref,
     m_ref, l_ref, acc_ref, carry_ref, sacc_ref) = rest[4 * n_pg:]
    g = pl.program_id(1)
    q = q_ref[...]
    qr = qr_ref[...]
    qbd = qbd_ref[...]
    page = nk_ref.shape[0] // kv_heads

    def heads_on_lanes(ref):
        return jnp.concatenate([ref[pl.ds(c, page, stride=kv_heads), :] for c in range(kv_heads)], axis=1).astype(BF)

    def sb_step(k, v, mask):
        z = _dot_nt(qbd, k) * sb_scale
        o, carry = _sb_block(z, mask, carry_ref[...], v)
        sacc_ref[...] += o
        carry_ref[...] = carry

    @pl.when(g == 0)
    def _():
        m_ref[...] = jnp.full(m_ref.shape, NEG_INF, F32)
        l_ref[...] = jnp.zeros(l_ref.shape, F32)
        acc_ref[...] = jnp.zeros(acc_ref.shape, F32)
        carry_ref[...] = jnp.zeros(carry_ref.shape, F32)
        sacc_ref[...] = jnp.zeros(sacc_ref.shape, F32)
        shape = (qbd.shape[0], page)
        row = lax.broadcasted_iota(jnp.int32, shape, 0)
        col = lax.broadcasted_iota(jnp.int32, shape, 1)
        sb_step(heads_on_lanes(nk_ref), heads_on_lanes(nv_ref), col < (row & (dec_seq - 1)))

    def update(s, kvs):
        m_prev = m_ref[...]
        m_new = jnp.maximum(m_prev, jnp.max(s, axis=-1, keepdims=True))
        alpha = jnp.exp(m_prev - m_new)
        p = jnp.exp(s - m_new)
        l_ref[...] = alpha * l_ref[...] + jnp.sum(p, axis=-1, keepdims=True)
        p = p.astype(BF)
        n = kvs[0].shape[0]
        pv = sum(_dot(p[:, t * n:(t + 1) * n], kvs[t]) for t in range(len(kvs)))
        acc_ref[...] = alpha * acc_ref[...] + pv
        m_ref[...] = m_new

    kvs = [r[...].astype(BF) for r in kv_refs]
    s = jnp.concatenate([_dot_nt(q, kvs[t]) + _dot(qr, kr_refs[t][...].astype(BF)) for t in range(n_pg)],
                        axis=1) * mla_scale
    update(s, kvs)

    k = jnp.concatenate([heads_on_lanes(k_refs[t]) for t in reversed(range(n_pg))], axis=0)
    v = jnp.concatenate([heads_on_lanes(v_refs[t]) for t in reversed(range(n_pg))], axis=0)
    o, carry = _sb_unmasked_blocks(_dot_nt(qbd, k) * sb_scale, carry_ref[...], v, 2 * page)
    sacc_ref[...] += o
    carry_ref[...] = carry

    @pl.when(g == pl.num_programs(1) - 1)
    def _():
        nkv = nkv_ref[...].astype(BF)
        sn = (_dot_nt(q, nkv) + _dot(qr, nkr_ref[...].astype(BF))) * mla_scale
        row = lax.broadcasted_iota(jnp.int32, sn.shape, 0)
        col = lax.broadcasted_iota(jnp.int32, sn.shape, 1)
        update(jnp.where(col <= (row & (dec_seq - 1)), sn, NEG_INF), [nkv])
        o_ref[...] = (acc_ref[...] * (1.0 / l_ref[...])).astype(o_ref.dtype)
        rows = sacc_ref.shape[0] // kv_heads
        dh = sacc_ref.shape[1] // kv_heads
        for c in range(kv_heads):
            osb_ref[c] = sacc_ref[c * rows:(c + 1) * rows, c * dh:(c + 1) * dh].astype(osb_ref.dtype)


def _decode(page_table, q_s, qr_s, q_bd, cache_kv, cache_kr_t, cache_k, cache_v,
            new_kv, new_kr_t, new_k, new_v, layer, dec_seq, kv_heads, mla_scale, sb_scale):
    db, rows, r = q_s.shape
    rows2, width = q_bd.shape[1:]
    dh = width // kv_heads
    n_pages = page_table.shape[1]
    page = cache_kv.shape[2]
    rope = cache_kr_t.shape[2]
    prow = cache_k.shape[2]
    n_pg = min(DECODE_PAGES, n_pages)
    assert n_pages % n_pg == 0 and n_pg % 2 == 0 and dec_seq & (dec_seq - 1) == 0

    def fwd_page(shape, t):
        return pl.BlockSpec((None, None) + shape, lambda b, g, pt: (layer, pt[b * n_pages + g * n_pg + t], 0, 0))

    def rev_page(t):
        return pl.BlockSpec((None, None, prow, dh),
                            lambda b, g, pt: (layer, pt[b * n_pages + n_pages - 1 - (g * n_pg + t)], 0, 0))

    def per_seq(shape):
        return pl.BlockSpec((None,) + shape, lambda b, g, pt: (b,) + (0,) * len(shape))

    in_specs = ([per_seq((rows, r)), per_seq((rows, rope)), per_seq((rows2, width))]
                + [fwd_page((page, r), t) for t in range(n_pg)]
                + [fwd_page((rope, page), t) for t in range(n_pg)]
                + [rev_page(t) for t in range(n_pg)] + [rev_page(t) for t in range(n_pg)]
                + [per_seq((page, r)), per_seq((rope, page)), per_seq((prow, dh)), per_seq((prow, dh))])
    return pl.pallas_call(
        functools.partial(_decode_kernel, n_pg=n_pg, dec_seq=dec_seq, kv_heads=kv_heads,
                          mla_scale=mla_scale, sb_scale=sb_scale),
        grid_spec=pltpu.PrefetchScalarGridSpec(
            num_scalar_prefetch=1,
            grid=(db, n_pages // n_pg),
            in_specs=in_specs,
            out_specs=(per_seq((rows, r)), per_seq((kv_heads, rows2 // kv_heads, dh))),
            scratch_shapes=[pltpu.VMEM((rows, 1), F32), pltpu.VMEM((rows, 1), F32), pltpu.VMEM((rows, r), F32),
                            pltpu.VMEM((rows2, 1), F32), pltpu.VMEM((rows2, width), F32)]),
        out_shape=(jax.ShapeDtypeStruct((db, rows, r), BF),
                   jax.ShapeDtypeStruct((db, kv_heads, rows2 // kv_heads, dh), BF)),
        compiler_params=_cp("parallel", "arbitrary"), name="decode",
    )(page_table.reshape(-1), q_s, qr_s, q_bd, *([cache_kv] * n_pg), *([cache_kr_t] * n_pg),
      *([cache_k] * n_pg), *([cache_v] * n_pg), new_kv, new_kr_t, new_k, new_v)


def _merge_kernel(vm_ref, os_ref, wm_ref, ws_ref, ga_ref, gb_ref, u_ref):
    ya = _dot(vm_ref[...], wm_ref[...].astype(BF))
    yb = _dot(os_ref[...], ws_ref[...].astype(BF))
    u_ref[...] = (ga_ref[...] * ya + gb_ref[...] * yb).astype(u_ref.dtype)


def _merge(v_mla, o_sb, w_branch_mla, w_branch_sb, gates, tm, tn):
    t, ka = v_mla.shape
    kb = o_sb.shape[1]
    d = w_branch_mla.shape[1]
    nj = d // tn
    return pl.pallas_call(
        _merge_kernel,
        grid=(t // tm, nj),
        in_specs=[pl.BlockSpec((tm, ka), lambda i, j: (i, 0)),
                  pl.BlockSpec((tm, kb), lambda i, j: (i, 0)),
                  pl.BlockSpec((ka, tn), lambda i, j: (0, j)),
                  pl.BlockSpec((kb, tn), lambda i, j: (0, j)),
                  pl.BlockSpec((tm, tn), lambda i, j: (i, j)),
                  pl.BlockSpec((tm, tn), lambda i, j: (i, nj + j))],
        out_specs=pl.BlockSpec((tm, tn), lambda i, j: (i, j)),
        out_shape=jax.ShapeDtypeStruct((t, d), BF),
        compiler_params=_cp("parallel", "arbitrary"), name="merge",
    )(v_mla, o_sb, w_branch_mla, w_branch_sb, gates, gates)


def _router_kernel(x_ref, g_ref, w_ref, b_ref, idx_ref, gate_ref, h_ref):
    h = _rms(x_ref[...]) * g_ref[...]
    h_ref[...] = h
    logits = jnp.dot(h, w_ref[...], preferred_element_type=F32, precision=lax.Precision.HIGHEST) + b_ref[...]
    n_exp = logits.shape[1]
    col = lax.broadcasted_iota(jnp.int32, logits.shape, 1)
    slot = lax.broadcasted_iota(jnp.int32, idx_ref.shape, 1)
    idx_out = jnp.zeros(idx_ref.shape, jnp.int32)
    val_out = jnp.zeros(gate_ref.shape, F32)
    top = None
    for k in range(TOP_K):
        m = jnp.max(logits, axis=-1, keepdims=True)
        idx = jnp.min(jnp.where(logits == m, col, n_exp), axis=-1, keepdims=True)
        top = m if top is None else top
        idx_out = jnp.where(slot == k, idx, idx_out)
        val_out = jnp.where(slot == k, jnp.exp(m - top), val_out)
        logits = jnp.where(col == idx, -jnp.inf, logits)
    idx_ref[...] = idx_out
    gate_ref[...] = val_out / jnp.sum(val_out, axis=-1, keepdims=True)


def _router(x, g, w_router, b_router, tm):
    t, d = x.shape
    n_exp = w_router.shape[1]
    return pl.pallas_call(
        _router_kernel,
        grid=(t // tm,),
        in_specs=[pl.BlockSpec((tm, d), lambda i: (i, 0)),
                  pl.BlockSpec((1, d), lambda i: (0, 0)),
                  pl.BlockSpec((d, n_exp), lambda i: (0, 0)),
                  pl.BlockSpec((1, n_exp), lambda i: (0, 0))],
        out_specs=(pl.BlockSpec((tm, TOP_K), lambda i: (i, 0)), pl.BlockSpec((tm, TOP_K), lambda i: (i, 0)),
                   pl.BlockSpec((tm, d), lambda i: (i, 0))),
        out_shape=(jax.ShapeDtypeStruct((t, TOP_K), jnp.int32), jax.ShapeDtypeStruct((t, TOP_K), F32),
                   jax.ShapeDtypeStruct((t, d), F32)),
        compiler_params=_cp("parallel"), name="router",
    )(x, g.reshape(1, d), w_router, b_router.reshape(1, n_exp))


ROW_CHUNK = 16


def _row_copy(src_hbm, row, dst, dst_row, sem):
    return pltpu.make_async_copy(src_hbm.at[pl.ds(row, 1), :], dst.at[pl.ds(dst_row, 1), :], sem)


def _start_rows(src_hbm, idx_ref, idx0, idx_stride, n, dst, dst_row0, sem):
    def issue(r2, c):
        for p in range(2):
            r = 2 * r2 + p
            _row_copy(src_hbm, idx_ref[idx0 + r * idx_stride], dst, dst_row0 + r, sem).start(priority=p)
        return c

    assert n % 2 == 0
    lax.fori_loop(0, n // 2, issue, 0, unroll=4)


def _wait_rows(src_hbm, dst, sem):
    pltpu.make_async_copy(src_hbm.at[pl.ds(0, dst.shape[0]), :], dst, sem).wait()


def _gather_rows_kernel(tok_ref, x_hbm, o_ref, buf, sem):
    tm = o_ref.shape[0]
    i = pl.program_id(0)
    slot = i % 2

    def start(block, s):
        _start_rows(x_hbm, tok_ref, block * tm, 1, tm, buf.at[s], 0, sem.at[s])

    @pl.when(i == 0)
    def _():
        start(0, 0)

    @pl.when(i + 1 < pl.num_programs(0))
    def _():
        start(i + 1, 1 - slot)

    _wait_rows(x_hbm, buf.at[slot], sem.at[slot])

    def chunk(c, carry):
        rows = pl.ds(pl.multiple_of(c * ROW_CHUNK, ROW_CHUNK), ROW_CHUNK)
        o_ref[rows, :] = buf[slot, rows, :].astype(o_ref.dtype)
        return carry

    lax.fori_loop(0, tm // ROW_CHUNK, chunk, 0, unroll=2)


def _gather_rows(x, row_tok, tm):
    t, d = x.shape
    n_rows = row_tok.shape[0]
    return pl.pallas_call(
        _gather_rows_kernel,
        grid_spec=pltpu.PrefetchScalarGridSpec(
            num_scalar_prefetch=1,
            grid=(n_rows // tm,),
            in_specs=[pl.BlockSpec(memory_space=pl.ANY)],
            out_specs=pl.BlockSpec((tm, d), lambda i, tok: (i, 0)),
            scratch_shapes=[pltpu.VMEM((2, tm, d), F32), pltpu.SemaphoreType.DMA((2,))]),
        out_shape=jax.ShapeDtypeStruct((n_rows, d), BF),
        compiler_params=_cp("arbitrary"), name="gather_rows",
    )(row_tok, x)


def _expert_rows(first_ref, count_ref, x_hbm, o_hbm, xbuf, obuf, sem_in, sem_out, compute):
    j, e = pl.program_id(0), pl.program_id(1)
    n_e = pl.num_programs(1)
    n = count_ref[e]
    row0 = first_ref[e]
    tm, tn = obuf.shape[1:]
    col0 = pl.multiple_of(j * tn, tn)

    def rows(first, b):
        return pl.ds(pl.multiple_of((first + b) * tm, tm), tm)

    def x_copy(first, b, slot):
        return pltpu.make_async_copy(x_hbm.at[rows(first, b), :], xbuf.at[slot], sem_in.at[slot])

    def o_copy(b, slot):
        return pltpu.make_async_copy(obuf.at[slot], o_hbm.at[rows(row0, b), pl.ds(col0, tn)], sem_out.at[slot])

    def start_head(first, count):
        for b in range(2):
            @pl.when(count > b)
            def _():
                x_copy(first, b, b).start()

    @pl.when((j == 0) & (e == 0))
    def _():
        start_head(row0, n)

    def body(b, carry):
        slot = b % 2
        x_copy(row0, b, slot).wait()

        @pl.when(b >= 2)
        def _():
            o_copy(b - 2, slot).wait()

        obuf[slot] = compute(xbuf[slot])
        o_copy(b, slot).start()

        @pl.when(b + 2 < n)
        def _():
            x_copy(row0, b + 2, slot).start()

        return carry

    lax.fori_loop(0, n, body, 0)

    @pl.when(n >= 2)
    def _():
        o_copy(n - 2, n % 2).wait()

    @pl.when(n >= 1)
    def _():
        o_copy(n - 1, (n - 1) % 2).wait()

    @pl.when((j + 1 < pl.num_programs(0)) | (e + 1 < n_e))
    def _():
        e_next = jnp.where(e + 1 < n_e, e + 1, 0)
        start_head(first_ref[e_next], count_ref[e_next])


def _expert_up_kernel(first_ref, count_ref, x_hbm, wg_ref, wu_ref, bg_ref, bu_ref, h_hbm, xbuf, hbuf, sem_in, sem_out):
    def compute(x):
        g = _dot(x, wg_ref[...].astype(BF)) + bg_ref[...]
        u = _dot(x, wu_ref[...].astype(BF)) + bu_ref[...]
        g = jnp.minimum(g, SWIGLU_LIMIT)
        u = jnp.clip(u, -SWIGLU_LIMIT, SWIGLU_LIMIT)
        return (g * jax.nn.sigmoid(SWIGLU_ALPHA * g) * (u + 1.0)).astype(hbuf.dtype)

    _expert_rows(first_ref, count_ref, x_hbm, h_hbm, xbuf, hbuf, sem_in, sem_out, compute)


def _expert_up(xs, first_block, n_blocks, w_gate_up, b_gate_up, layer, tm, tn):
    n_rows, d = xs.shape
    d_ff = w_gate_up.shape[3] // 2
    nj = d_ff // tn
    n_exp = w_gate_up.shape[1]
    b3 = b_gate_up.reshape(b_gate_up.shape[0], n_exp, 1, 2 * d_ff)
    return pl.pallas_call(
        _expert_up_kernel,
        grid_spec=pltpu.PrefetchScalarGridSpec(
            num_scalar_prefetch=2,
            grid=(nj, n_exp),
            in_specs=[pl.BlockSpec(memory_space=pl.ANY),
                      pl.BlockSpec((None, None, d, tn), lambda j, e, fb, nb: (layer, e, 0, j)),
                      pl.BlockSpec((None, None, d, tn), lambda j, e, fb, nb: (layer, e, 0, nj + j)),
                      pl.BlockSpec((None, None, 1, tn), lambda j, e, fb, nb: (layer, e, 0, j)),
                      pl.BlockSpec((None, None, 1, tn), lambda j, e, fb, nb: (layer, e, 0, nj + j))],
            out_specs=pl.BlockSpec(memory_space=pl.ANY),
            scratch_shapes=[pltpu.VMEM((2, tm, d), BF), pltpu.VMEM((2, tm, tn), BF),
                            pltpu.SemaphoreType.DMA((2,)), pltpu.SemaphoreType.DMA((2,))]),
        out_shape=jax.ShapeDtypeStruct((n_rows, d_ff), BF),
        compiler_params=_cp("arbitrary", "arbitrary"), name="expert_up",
    )(first_block, n_blocks, xs, w_gate_up, w_gate_up, b3, b3)


def _expert_down_kernel(first_ref, count_ref, h_hbm, w_ref, b_ref, y_hbm, hbuf, ybuf, sem_in, sem_out):
    def compute(h):
        return _dot(h, w_ref[...].astype(BF)) + b_ref[...]

    _expert_rows(first_ref, count_ref, h_hbm, y_hbm, hbuf, ybuf, sem_in, sem_out, compute)


def _expert_down(hs, first_block, n_blocks, w_down, b_down, layer, tm, tn):
    n_rows, d_ff = hs.shape
    n_exp, _, d = w_down.shape[1:]
    b3 = b_down.reshape(b_down.shape[0], n_exp, 1, d)
    return pl.pallas_call(
        _expert_down_kernel,
        grid_spec=pltpu.PrefetchScalarGridSpec(
            num_scalar_prefetch=2,
            grid=(d // tn, n_exp),
            in_specs=[pl.BlockSpec(memory_space=pl.ANY),
                      pl.BlockSpec((None, None, d_ff, tn), lambda j, e, fb, nb: (layer, e, 0, j)),
                      pl.BlockSpec((None, None, 1, tn), lambda j, e, fb, nb: (layer, e, 0, j))],
            out_specs=pl.BlockSpec(memory_space=pl.ANY),
            scratch_shapes=[pltpu.VMEM((2, tm, d_ff), BF), pltpu.VMEM((2, tm, tn), F32),
                            pltpu.SemaphoreType.DMA((2,)), pltpu.SemaphoreType.DMA((2,))]),
        out_shape=jax.ShapeDtypeStruct((n_rows, d), F32),
        compiler_params=_cp("arbitrary", "arbitrary"), name="expert_down",
    )(first_block, n_blocks, hs, w_down, b3)


def _combine_kernel(dest_ref, y_hbm, gate_ref, x_ref, g_ref, op_ref, os_ref, buf, sem, *, n_first):
    tc = x_ref.shape[0]
    i = pl.program_id(0)
    slot = i % 2

    def start(block, s):
        for k in range(TOP_K):
            _start_rows(y_hbm, dest_ref, block * tc * TOP_K + k, TOP_K, tc, buf.at[s], k * tc, sem.at[s])

    @pl.when(i == 0)
    def _():
        start(0, 0)

    @pl.when(i + 1 < pl.num_programs(0))
    def _():
        start(i + 1, 1 - slot)

    _wait_rows(y_hbm, buf.at[slot], sem.at[slot])

    def write(o_ref):
        def chunk(c, carry):
            rows = pl.ds(pl.multiple_of(c * 8, 8), 8)
            gate = gate_ref[rows, :]
            moe = sum(buf[slot, pl.ds(pl.multiple_of(k * tc + c * 8, 8), 8), :] * gate[:, k:k + 1]
                      for k in range(TOP_K))
            o_ref[rows, :] = _rms(x_ref[rows, :] + moe) * g_ref[...]
            return carry

        lax.fori_loop(0, tc // 8, chunk, 0, unroll=2)

    @pl.when(i < n_first)
    def _():
        write(op_ref)

    @pl.when(i >= n_first)
    def _():
        write(os_ref)


def _combine(y_rows, dest, gate, x, g_final, tc, t_first):
    t, d = x.shape
    n_first = t_first // tc
    assert t_first % tc == 0 and 0 < n_first < t // tc
    return pl.pallas_call(
        functools.partial(_combine_kernel, n_first=n_first),
        grid_spec=pltpu.PrefetchScalarGridSpec(
            num_scalar_prefetch=1,
            grid=(t // tc,),
            in_specs=[pl.BlockSpec(memory_space=pl.ANY),
                      pl.BlockSpec((tc, TOP_K), lambda i, dst: (i, 0)),
                      pl.BlockSpec((tc, d), lambda i, dst: (i, 0)),
                      pl.BlockSpec((1, d), lambda i, dst: (0, 0))],
            out_specs=(pl.BlockSpec((tc, d), lambda i, dst: (jnp.minimum(i, n_first - 1), 0)),
                       pl.BlockSpec((tc, d), lambda i, dst: (jnp.maximum(i - n_first, 0), 0))),
            scratch_shapes=[pltpu.VMEM((2, TOP_K * tc, d), F32), pltpu.SemaphoreType.DMA((2,))]),
        out_shape=(jax.ShapeDtypeStruct((t_first, d), F32), jax.ShapeDtypeStruct((t - t_first, d), F32)),
        compiler_params=_cp("arbitrary"), name="combine",
    )(dest, y_rows, gate, x, g_final.reshape(1, d))


def _routing_tables(top_idx, n_exp, tm):
    n_pairs = top_idx.size
    flat_e = top_idx.reshape(-1)
    onehot = (flat_e[:, None] == jnp.arange(n_exp, dtype=jnp.int32)[None, :]).astype(jnp.int32)
    csum = jnp.cumsum(onehot, axis=0)
    counts = csum[-1]
    padded = (counts + tm - 1) // tm * tm
    pad_ends = jnp.cumsum(padded)
    pad_starts = pad_ends - padded
    dest = jnp.sum(onehot * (pad_starts[None, :] + csum - 1), axis=1).astype(jnp.int32)
    n_rows = (n_pairs + n_exp * (tm - 1) + tm - 1) // tm * tm
    row_tok = (jnp.arange(n_rows, dtype=jnp.int32) % (n_pairs // TOP_K)).at[dest].set(
        jnp.arange(n_pairs, dtype=jnp.int32) // TOP_K)
    return dest, row_tok, (pad_starts // tm).astype(jnp.int32), (padded // tm).astype(jnp.int32)


def _pick(n, pref):
    b = min(pref, n)
    while n % b:
        b //= 2
    return b


def kernel(x_prompt, x_sample, cache_kv_latent, cache_k_rope, cache_sb_k, cache_sb_v, page_table, norm_attn, w_in, b_gate, q_a_norm, w_uq, kv_a_norm, w_ukv, w_branch_mla, w_branch_sb, w_out, norm_ffn, w_router, b_router, w_gate_up, b_gate_up, w_down, b_down, norm_final):
    batch, seq, d = x_prompt.shape
    db, dec_seq, _ = x_sample.shape
    depth = w_in.shape[0]
    q_rank = q_a_norm.shape[1]
    kv_rank = kv_a_norm.shape[1]
    rope = cache_k_rope.shape[3]
    heads = w_ukv.shape[2]
    nope = w_uq.shape[2] // heads - rope
    page = cache_kv_latent.shape[2]
    kv_heads, dh = cache_sb_k.shape[3], cache_sb_k.shape[4]
    n_pages = page_table.shape[1]
    past_len = n_pages * page
    in_cols = w_in.shape[2]
    sb_heads = (in_cols - q_rank - kv_rank - rope - 2 * d) // dh - 2 * kv_heads
    group = sb_heads // kv_heads
    n_exp = w_router.shape[2]
    mla_scale = float(nope + rope) ** -0.5
    sb_scale = float(dh) ** -0.5
    assert rope == 64 and dh == LANES and kv_rank % LANES == 0 and (q_rank + kv_rank) % LANES == 0

    tp, ts = batch * seq, db * dec_seq
    t = tp + ts
    x = jnp.concatenate([x_prompt.reshape(tp, d), x_sample.reshape(ts, d)], axis=0)

    pos = jnp.concatenate([jnp.tile(jnp.arange(seq), batch), jnp.tile(past_len + jnp.arange(dec_seq), db)])
    inv_freq = 1.0 / (ROPE_THETA ** (jnp.arange(0, rope, 2, dtype=F32) / rope))
    ang = pos.astype(F32)[:, None] * inv_freq[None, :]
    cos, sin = jnp.cos(ang), jnp.sin(ang)
    cosf = jnp.concatenate([cos, cos, cos, cos], axis=1)
    sinf = jnp.concatenate([-sin, sin, -sin, sin], axis=1)

    tm_big = _pick(t, 1024)
    tm_mid = _pick(t, 512)
    outs_p, outs_s = [], []
    for l in range(depth):
        rest0 = q_rank + kv_rank + rope
        w_in_t = jnp.swapaxes(w_in, 1, 2)[l]
        w_rest = w_in_t[rest0:]
        w_uq3 = w_uq[l].reshape(q_rank, heads, nope + rope)
        w_nope = jnp.transpose(w_uq3[:, :, :nope], (1, 0, 2))
        w_rope_pair = jnp.transpose(w_uq3[:, :, nope:].reshape(q_rank, heads // 2, 2 * rope), (1, 0, 2))
        w_kt = jnp.transpose(w_ukv[l][:, :, :nope], (1, 2, 0))
        w_v = jnp.transpose(w_ukv[l][:, :, nope:], (1, 0, 2))

        h = _rmsnorm(x, norm_attn[l], BF, _pick(t, 256))
        cq, ckv = _latents(h, w_in_t, q_a_norm[l], kv_a_norm[l], q_rank, kv_rank, tm_mid)
        kr, krd = _rope_key(h, w_in_t, (q_rank + kv_rank) // LANES, cosf, sinf, rope, tm_big)
        c0 = sb_heads * dh
        tn = _pick(d, 512)
        assert c0 % tn == 0 and (c0 + 2 * kv_heads * dh) % tn == 0 and c0 % (kv_heads * dh) == 0
        sb_q = _plain_proj(h, w_rest, 0, c0 // tn, tm_big, tn, BF)
        sb_kv = _plain_proj(h, w_rest, c0 // (kv_heads * dh), 2, tm_big, kv_heads * dh, F32)
        sb_k, sb_v = sb_kv[:, :kv_heads * dh], sb_kv[:, kv_heads * dh:]
        gates = _gates(h, w_rest, (c0 + 2 * kv_heads * dh) // tn, b_gate[l], tm_big, tn)
        q_lat, q_rope = _mla_queries(cq, w_nope, w_rope_pair, w_kt, cosf, sinf, tm_big)

        o_lat_p = _mla_prompt(q_lat, q_rope, ckv, krd, batch, seq, min(128, seq), min(512, seq), mla_scale)
        o_sb_p = _sb_prompt(sb_q, sb_k, sb_v, batch, seq, kv_heads, min(128, seq), min(256, seq), sb_scale)

        def seq_major(a):
            return jnp.transpose(a[:, tp:].reshape(a.shape[0], db, dec_seq, a.shape[2]), (1, 0, 2, 3)).reshape(
                db, a.shape[0] * dec_seq, a.shape[2])

        def new_page(a, rows_per_pos):
            a = a[tp:].reshape(db, dec_seq * rows_per_pos, a.shape[1] // rows_per_pos)
            return jnp.pad(a, ((0, 0), (0, (page - dec_seq) * rows_per_pos), (0, 0)))

        q_s = seq_major(q_lat)
        qr_m = seq_major(q_rope)
        qr_s = qr_m[:, :, :rope] + qr_m[:, :, rope:]
        sbq_s = jnp.transpose(sb_q[tp:].reshape(db, dec_seq, kv_heads, group, dh), (0, 2, 3, 1, 4)).reshape(
            db, kv_heads, group * dec_seq, dh)
        q_bd = jnp.concatenate(
            [jnp.concatenate([sbq_s[:, c] if c2 == c else jnp.zeros_like(sbq_s[:, c]) for c2 in range(kv_heads)], axis=2)
             for c in range(kv_heads)], axis=1)
        cache_k2 = cache_sb_k.reshape(depth, -1, page * kv_heads, dh)
        cache_v2 = cache_sb_v.reshape(depth, -1, page * kv_heads, dh)
        cache_kr_t = jnp.swapaxes(cache_k_rope, 2, 3)
        o_lat_s, o_sb_s = _decode(page_table, q_s, qr_s, q_bd, cache_kv_latent, cache_kr_t, cache_k2, cache_v2,
                                  new_page(ckv, 1), jnp.swapaxes(new_page(kr, 1), 1, 2),
                                  new_page(sb_k, kv_heads), new_page(sb_v, kv_heads),
                                  l, dec_seq, kv_heads, mla_scale, sb_scale)
        o_lat_s = jnp.transpose(o_lat_s.reshape(db, heads, dec_seq, kv_rank), (1, 0, 2, 3)).reshape(heads, ts, kv_rank)
        o_sb_s = jnp.transpose(o_sb_s.reshape(db, kv_heads, group, dec_seq, dh), (0, 3, 1, 2, 4)).reshape(ts, sb_heads * dh)

        v_mla = jnp.concatenate([_value_up(o_lat_p, w_v, _pick(tp, 1024)), _value_up(o_lat_s, w_v, _pick(ts, 1024))], axis=0)
        o_sb = jnp.concatenate([o_sb_p, o_sb_s], axis=0)
        u = _merge(v_mla, o_sb, w_branch_mla[l], w_branch_sb[l], gates, tm_mid, tn)
        x = _out_proj(u, w_out[l], x, tm_big, tn)

        top_idx, gate, h_ffn = _router(x, norm_ffn[l], w_router[l], b_router[l], _pick(t, 256))
        dest, row_tok, first_block, n_blocks = _routing_tables(top_idx, n_exp, MOE_BLOCK)
        xs = _gather_rows(h_ffn, row_tok, MOE_BLOCK)
        hs = _expert_up(xs, first_block, n_blocks, w_gate_up, b_gate_up, l, MOE_BLOCK, _pick(w_gate_up.shape[3] // 2, 512))
        ys = _expert_down(hs, first_block, n_blocks, w_down, b_down, l, MOE_BLOCK, _pick(d, 1024))
        last = l == depth - 1
        assert last, "a deeper stack needs a combine variant without the final norm"
        y_p, y_s = _combine(ys, dest, gate, x, norm_final, _pick(ts, 64), tp)

        outs_p.append((ckv[:tp].reshape(batch, seq, kv_rank), kr[:tp].reshape(batch, seq, rope),
                       sb_k[:tp].reshape(batch, seq, kv_heads, dh), sb_v[:tp].reshape(batch, seq, kv_heads, dh)))
        outs_s.append((ckv[tp:].reshape(db, dec_seq, kv_rank), kr[tp:].reshape(db, dec_seq, rope),
                       sb_k[tp:].reshape(db, dec_seq, kv_heads, dh), sb_v[tp:].reshape(db, dec_seq, kv_heads, dh)))

    y_prompt = y_p.reshape(batch, seq, d)
    y_sample = y_s.reshape(db, dec_seq, d)
    stack = lambda rows, k: jnp.stack([r[k] for r in rows])
    return (y_prompt, y_sample) + tuple(stack(outs_p, k) for k in range(4)) + tuple(stack(outs_s, k) for k in range(4))
```

```python
import functools

import jax
import jax.numpy as jnp
from jax import lax
from jax.experimental import pallas as pl
from jax.experimental.pallas import tpu as pltpu

TOP_K = 4
SWIGLU_LIMIT = 7.0
SWIGLU_ALPHA = 1.702
EPS = 1e-6
NEG_INF = -1e30
ROPE_THETA = 10000.0

LANES = 128
VMEM_LIMIT = 56 * 1024 * 1024
MOE_BLOCK = 256
DECODE_PAGES = 16

BF = jnp.bfloat16
F32 = jnp.float32


def _cp(*sem):
    return pltpu.CompilerParams(dimension_semantics=sem, vmem_limit_bytes=VMEM_LIMIT)


def _dot(a, b):
    return jnp.dot(a, b, preferred_element_type=F32)


def _dot_nt(a, b):
    return lax.dot_general(a, b, (((1,), (1,)), ((), ())), preferred_element_type=F32)


def _rms(x):
    return x * lax.rsqrt(jnp.mean(x * x, axis=-1, keepdims=True) + EPS)


def _rope128(x, cosf, sinf):
    lane = lax.broadcasted_iota(jnp.int32, x.shape, 1)
    rot = jnp.where((lane & 63) < 32, pltpu.roll(x, 96, 1), pltpu.roll(x, 32, 1))
    return x * cosf + rot * sinf


def _rmsnorm_kernel(x_ref, g_ref, o_ref):
    o_ref[...] = (_rms(x_ref[...]) * g_ref[...]).astype(o_ref.dtype)


def _rmsnorm(x, g, out_dtype, tm):
    t, d = x.shape
    return pl.pallas_call(
        _rmsnorm_kernel,
        grid=(t // tm,),
        in_specs=[pl.BlockSpec((tm, d), lambda i: (i, 0)), pl.BlockSpec((1, d), lambda i: (0, 0))],
        out_specs=pl.BlockSpec((tm, d), lambda i: (i, 0)),
        out_shape=jax.ShapeDtypeStruct((t, d), out_dtype),
        compiler_params=_cp("parallel"), name="rmsnorm",
    )(x, g.reshape(1, d))


def _mm_kernel(*refs, n_extra, epilogue, n_scratch, w_is_nk):
    a_ref, w_ref = refs[0], refs[1]
    extras = refs[2:2 + n_extra]
    rest = refs[2 + n_extra:]
    outs = rest[:len(rest) - n_scratch]
    scratch = rest[len(rest) - n_scratch:]
    w = w_ref[...].astype(BF)
    acc = _dot_nt(a_ref[...], w) if w_is_nk else _dot(a_ref[...], w)
    epilogue(acc, extras, outs, scratch)


def _mm(name, a, w, col_block0, n_tiles, tm, tn, epilogue, out_shape, out_specs,
        extras=(), extra_specs=(), scratch_shapes=(), w_is_nk=False):
    m, k = a.shape
    if w_is_nk:
        w_spec = pl.BlockSpec((tn, k), lambda i, j: (col_block0 + j, 0))
    else:
        w_spec = pl.BlockSpec((k, tn), lambda i, j: (0, col_block0 + j))
    in_specs = [pl.BlockSpec((tm, k), lambda i, j: (i, 0)), w_spec] + list(extra_specs)
    kern = functools.partial(_mm_kernel, n_extra=len(extras), epilogue=epilogue,
                             n_scratch=len(scratch_shapes), w_is_nk=w_is_nk)
    return pl.pallas_call(
        kern,
        grid=(m // tm, n_tiles),
        in_specs=in_specs,
        out_specs=out_specs,
        out_shape=out_shape,
        scratch_shapes=list(scratch_shapes),
        compiler_params=_cp("parallel", "arbitrary"), name=name,
    )(a, w, *extras)


def _tile_spec(tm, tn, off=0):
    return pl.BlockSpec((tm, tn), lambda i, j: (i, off + j))


def _row_spec(tm, c):
    return pl.BlockSpec((tm, c), lambda i, j: (i, 0))


def _col_spec(tn, off=0):
    return pl.BlockSpec((1, tn), lambda i, j: (0, off + j))


def _latents(h, w_in, q_a_norm, kv_a_norm, q_rank, kv_rank, tm):
    t = h.shape[0]
    tn = kv_rank
    nq = q_rank // tn

    def epilogue(acc, extras, outs, scratch):
        gq_ref, gkv_ref = extras
        cq_ref, ckv_ref = outs
        buf, = scratch
        j = pl.program_id(1)
        buf[j] = acc

        @pl.when(j == nq)
        def _():
            ssq = sum(jnp.sum(buf[s] * buf[s], axis=-1, keepdims=True) for s in range(nq))
            rs = lax.rsqrt(ssq / q_rank + EPS)
            for s in range(nq):
                cq_ref[:, s * tn:(s + 1) * tn] = (buf[s] * rs * gq_ref[:, s * tn:(s + 1) * tn]).astype(BF)
            ckv_ref[...] = _rms(buf[nq]) * gkv_ref[...]

    return _mm(
        "latents", h, w_in, 0, nq + 1, tm, tn, epilogue,
        out_shape=(jax.ShapeDtypeStruct((t, q_rank), BF), jax.ShapeDtypeStruct((t, kv_rank), F32)),
        out_specs=(_row_spec(tm, q_rank), _row_spec(tm, kv_rank)),
        extras=(q_a_norm.reshape(1, -1), kv_a_norm.reshape(1, -1)),
        extra_specs=(pl.BlockSpec((1, q_rank), lambda i, j: (0, 0)),
                     pl.BlockSpec((1, kv_rank), lambda i, j: (0, 0))),
        scratch_shapes=(pltpu.VMEM((nq + 1, tm, tn), F32),), w_is_nk=True)


def _rope_key(h, w_in, col_block0, cosf, sinf, rope_dim, tm):
    t = h.shape[0]

    def epilogue(acc, extras, outs, scratch):
        cos_ref, sin_ref = extras
        kr_ref, krd_ref = outs
        r = _rope128(acc, cos_ref[...], sin_ref[...])
        kr_ref[...] = r[:, :rope_dim]
        lane = lax.broadcasted_iota(jnp.int32, r.shape, 1)
        krd_ref[...] = jnp.where(lane < rope_dim, r, pltpu.roll(r, rope_dim, 1)).astype(BF)

    return _mm(
        "rope_key", h, w_in, col_block0, 1, tm, LANES, epilogue,
        out_shape=(jax.ShapeDtypeStruct((t, rope_dim), F32), jax.ShapeDtypeStruct((t, LANES), BF)),
        out_specs=(_row_spec(tm, rope_dim), _row_spec(tm, LANES)),
        extras=(cosf, sinf), extra_specs=(_row_spec(tm, LANES), _row_spec(tm, LANES)), w_is_nk=True)


def _plain_proj(a, w, col_block0, n_tiles, tm, tn, out_dtype):
    def epilogue(acc, extras, outs, scratch):
        outs[0][...] = acc.astype(out_dtype)

    return _mm("plain_proj", a, w, col_block0, n_tiles, tm, tn, epilogue,
               out_shape=jax.ShapeDtypeStruct((a.shape[0], n_tiles * tn), out_dtype),
               out_specs=_tile_spec(tm, tn), w_is_nk=True)


def _gates(h, w, col_block0, b_gate, tm, tn):
    n = b_gate.shape[0]

    def epilogue(acc, extras, outs, scratch):
        outs[0][...] = jax.nn.sigmoid(acc + extras[0][...])

    return _mm("gates", h, w, col_block0, n // tn, tm, tn, epilogue,
               out_shape=jax.ShapeDtypeStruct((h.shape[0], n), F32),
               out_specs=_tile_spec(tm, tn),
               extras=(b_gate.reshape(1, n),), extra_specs=(_col_spec(tn),), w_is_nk=True)


def _out_proj(u, w_out, x, tm, tn):
    def epilogue(acc, extras, outs, scratch):
        outs[0][...] = extras[0][...] + acc

    t, d = x.shape
    return _mm("out_proj", u, w_out, 0, d // tn, tm, tn, epilogue,
               out_shape=jax.ShapeDtypeStruct((t, d), F32), out_specs=_tile_spec(tm, tn),
               extras=(x,), extra_specs=(_tile_spec(tm, tn),))


def _q_kernel(a_ref, wn_ref, wr_ref, wk_ref, cos_ref, sin_ref, ql_ref, qr_ref):
    a = a_ref[...]
    for s in range(2):
        qn = _dot(a, wn_ref[s].astype(BF)).astype(BF)
        ql_ref[s] = _dot(qn, wk_ref[s].astype(BF)).astype(BF)
    r = _rope128(_dot(a, wr_ref[...].astype(BF)), cos_ref[...], sin_ref[...])
    lane = lax.broadcasted_iota(jnp.int32, r.shape, 1)
    qr_ref[0] = jnp.where(lane < 64, r, 0.0).astype(BF)
    qr_ref[1] = jnp.where(lane >= 64, r, 0.0).astype(BF)


def _mla_queries(cq, w_nope, w_rope_pair, w_kt, cosf, sinf, tm):
    t, qr = cq.shape
    heads, _, nope = w_nope.shape
    kv_rank = w_kt.shape[2]
    return pl.pallas_call(
        _q_kernel,
        grid=(t // tm, heads // 2),
        in_specs=[pl.BlockSpec((tm, qr), lambda i, p: (i, 0)),
                  pl.BlockSpec((2, qr, nope), lambda i, p: (p, 0, 0)),
                  pl.BlockSpec((None, qr, LANES), lambda i, p: (p, 0, 0)),
                  pl.BlockSpec((2, nope, kv_rank), lambda i, p: (p, 0, 0)),
                  pl.BlockSpec((tm, LANES), lambda i, p: (i, 0)),
                  pl.BlockSpec((tm, LANES), lambda i, p: (i, 0))],
        out_specs=(pl.BlockSpec((2, tm, kv_rank), lambda i, p: (p, i, 0)),
                   pl.BlockSpec((2, tm, LANES), lambda i, p: (p, i, 0))),
        out_shape=(jax.ShapeDtypeStruct((heads, t, kv_rank), BF),
                   jax.ShapeDtypeStruct((heads, t, LANES), BF)),
        compiler_params=_cp("parallel", "arbitrary"), name="mla_queries",
    )(cq, w_nope, w_rope_pair, w_kt, cosf, sinf)


def _vup_kernel(o_ref, w_ref, out_ref):
    out_ref[...] = _dot(o_ref[...], w_ref[...].astype(BF)).astype(out_ref.dtype)


def _value_up(o_lat, w_v, tm):
    heads, t, r = o_lat.shape
    v_dim = w_v.shape[2]
    return pl.pallas_call(
        _vup_kernel,
        grid=(t // tm, heads),
        in_specs=[pl.BlockSpec((None, tm, r), lambda i, h: (h, i, 0)),
                  pl.BlockSpec((None, r, v_dim), lambda i, h: (h, 0, 0))],
        out_specs=pl.BlockSpec((tm, v_dim), lambda i, h: (i, h)),
        out_shape=jax.ShapeDtypeStruct((t, heads * v_dim), BF),
        compiler_params=_cp("parallel", "arbitrary"), name="value_up",
    )(o_lat, w_v)


def _mla_prompt_kernel(q_ref, qr_ref, kv_ref, kr_ref, o_ref, m_ref, l_ref, acc_ref, *, tq, tk, scale):
    i = pl.program_id(1)
    j = pl.program_id(2)
    heads = q_ref.shape[0]
    rows = heads * tq
    j_last = (i * tq + tq - 1) // tk

    @pl.when(j == 0)
    def _():
        m_ref[...] = jnp.full(m_ref.shape, NEG_INF, F32)
        l_ref[...] = jnp.zeros(l_ref.shape, F32)
        acc_ref[...] = jnp.zeros(acc_ref.shape, F32)

    def step(diagonal):
        q = q_ref[...].reshape(rows, q_ref.shape[2])
        qr = qr_ref[...].reshape(rows, qr_ref.shape[2])
        kv = kv_ref[...].astype(BF)
        s = (_dot_nt(q, kv) + _dot_nt(qr, kr_ref[...])) * scale
        if diagonal:
            q_pos = i * tq + (lax.broadcasted_iota(jnp.int32, s.shape, 0) & (tq - 1))
            k_pos = j * tk + lax.broadcasted_iota(jnp.int32, s.shape, 1)
            s = jnp.where(k_pos <= q_pos, s, NEG_INF)
        m_prev = m_ref[...]
        m_new = jnp.maximum(m_prev, jnp.max(s, axis=-1, keepdims=True))
        alpha = jnp.exp(m_prev - m_new)
        p = jnp.exp(s - m_new)
        l_ref[...] = alpha * l_ref[...] + jnp.sum(p, axis=-1, keepdims=True)
        acc_ref[...] = alpha * acc_ref[...] + _dot(p.astype(BF), kv)
        m_ref[...] = m_new

    @pl.when(j <= j_last)
    def _():
        step(True)

    @pl.when(j == j_last)
    def _():
        o = acc_ref[...] * (1.0 / l_ref[...])
        o_ref[...] = o.reshape(o_ref.shape).astype(o_ref.dtype)


def _mla_prompt(q_lat, q_rope, ckv, krd, batch, seq, tq, tk, scale):
    heads, _, r = q_lat.shape
    nq, nk = seq // tq, seq // tk
    assert tq & (tq - 1) == 0 and tk % tq == 0

    def kmap(b, i, j):
        return (b * nk + jnp.minimum(j, (i * tq + tq - 1) // tk), 0)

    return pl.pallas_call(
        functools.partial(_mla_prompt_kernel, tq=tq, tk=tk, scale=scale),
        grid=(batch, nq, nk),
        in_specs=[pl.BlockSpec((heads, tq, r), lambda b, i, j: (0, b * nq + i, 0)),
                  pl.BlockSpec((heads, tq, LANES), lambda b, i, j: (0, b * nq + i, 0)),
                  pl.BlockSpec((tk, r), kmap),
                  pl.BlockSpec((tk, LANES), kmap)],
        out_specs=pl.BlockSpec((heads, tq, r), lambda b, i, j: (0, b * nq + i, 0)),
        out_shape=jax.ShapeDtypeStruct((heads, batch * seq, r), BF),
        scratch_shapes=[pltpu.VMEM((heads * tq, 1), F32), pltpu.VMEM((heads * tq, 1), F32),
                        pltpu.VMEM((heads * tq, r), F32)],
        compiler_params=_cp("parallel", "parallel", "arbitrary"), name="mla_prompt",
    )(q_lat, q_rope, ckv, krd)


def _strict_upper(n):
    return (lax.broadcasted_iota(jnp.int32, (n, n), 0) > lax.broadcasted_iota(jnp.int32, (n, n), 1)).astype(BF)


def _sb_block(z, mask, carry, v):
    sp = jnp.log(1.0 + jnp.exp(-jnp.abs(z)))
    log_beta = jnp.minimum(z, 0.0) - sp
    log_keep = -jnp.maximum(z, 0.0) - sp
    if mask is not None:
        log_keep = jnp.where(mask, log_keep, 0.0)
    upper = _strict_upper(z.shape[1])
    hi = log_keep.astype(BF)
    lo = (log_keep - hi.astype(F32)).astype(BF)
    after = _dot(hi, upper) + _dot(lo, upper) + carry
    a = jnp.exp(log_beta + after)
    if mask is not None:
        a = jnp.where(mask, a, 0.0)
    return _dot(a.astype(BF), v), carry + jnp.sum(log_keep, axis=-1, keepdims=True)


def _sb_unmasked_blocks(z, carry, v, blk):
    n = z.shape[1]
    sp = jnp.log(1.0 + jnp.exp(-jnp.abs(z)))
    log_beta = jnp.minimum(z, 0.0) - sp
    log_keep = -jnp.maximum(z, 0.0) - sp
    upper = _strict_upper(blk)
    cols = [slice(b * blk, (b + 1) * blk) for b in range(n // blk)]
    sums = [jnp.sum(log_keep[:, c], axis=-1, keepdims=True) for c in cols]
    later = carry
    mass_after = [None] * len(cols)
    for b in reversed(range(len(cols))):
        mass_after[b] = later
        later = later + sums[b]
    parts = []
    for b, c in enumerate(cols):
        hi = log_keep[:, c].astype(BF)
        lo = (log_keep[:, c] - hi.astype(F32)).astype(BF)
        after = _dot(hi, upper) + _dot(lo, upper) + mass_after[b]
        parts.append(jnp.exp(log_beta[:, c] + after).astype(BF))
    return _dot(jnp.concatenate(parts, axis=1), v), later


def _sb_prompt_kernel(q_ref, k_ref, v_ref, o_ref, qrows_ref, carry_ref, acc_ref, *, tq, tk, group, scale):
    i = pl.program_id(2)
    jj = pl.program_id(3)
    dh = k_ref.shape[1]
    j_max = (i * tq + tq - 1) // tk

    @pl.when(jj == 0)
    def _():
        for g in range(group):
            qrows_ref[g * tq:(g + 1) * tq, :] = q_ref[:, g * dh:(g + 1) * dh]
        carry_ref[...] = jnp.zeros(carry_ref.shape, F32)
        acc_ref[...] = jnp.zeros(acc_ref.shape, F32)

    def step(diagonal):
        z = _dot_nt(qrows_ref[...], k_ref[...].astype(BF)) * scale
        v = v_ref[...].astype(BF)
        if diagonal:
            q_pos = i * tq + (lax.broadcasted_iota(jnp.int32, z.shape, 0) & (tq - 1))
            k_pos = j_max * tk + lax.broadcasted_iota(jnp.int32, z.shape, 1)
            o, carry = _sb_block(z, k_pos < q_pos, carry_ref[...], v)
        else:
            o, carry = _sb_unmasked_blocks(z, carry_ref[...], v, tk)
        acc_ref[...] += o
        carry_ref[...] = carry

    @pl.when(jj == 0)
    def _():
        step(True)

    @pl.when((jj > 0) & (jj <= j_max))
    def _():
        step(False)

    @pl.when(jj == j_max)
    def _():
        for g in range(group):
            o_ref[:, g * dh:(g + 1) * dh] = acc_ref[g * tq:(g + 1) * tq, :].astype(o_ref.dtype)


def _sb_prompt(sb_q, sb_k, sb_v, batch, seq, kv_heads, tq, tk, scale):
    dh = sb_k.shape[1] // kv_heads
    group = sb_q.shape[1] // (kv_heads * dh)
    nq, nk = seq // tq, seq // tk
    assert tq & (tq - 1) == 0 and tk % tq == 0

    def kmap(b, c, i, jj):
        return (b * nk + jnp.maximum((i * tq + tq - 1) // tk - jj, 0), c)

    return pl.pallas_call(
        functools.partial(_sb_prompt_kernel, tq=tq, tk=tk, group=group, scale=scale),
        grid=(batch, kv_heads, nq, nk),
        in_specs=[pl.BlockSpec((tq, group * dh), lambda b, c, i, jj: (b * nq + i, c)),
                  pl.BlockSpec((tk, dh), kmap),
                  pl.BlockSpec((tk, dh), kmap)],
        out_specs=pl.BlockSpec((tq, group * dh), lambda b, c, i, jj: (b * nq + i, c)),
        out_shape=jax.ShapeDtypeStruct((batch * seq, sb_q.shape[1]), BF),
        scratch_shapes=[pltpu.VMEM((group * tq, dh), BF), pltpu.VMEM((group * tq, 1), F32),
                        pltpu.VMEM((group * tq, dh), F32)],
        compiler_params=_cp("parallel", "parallel", "parallel", "arbitrary"), name="sb_prompt",
    )(sb_q, sb_k, sb_v)


def _decode_kernel(pt_ref, q_ref, qr_ref, qbd_ref, *rest, n_pg, dec_seq, kv_heads, mla_scale, sb_scale):
    kv_refs = rest[:n_pg]
    kr_refs = rest[n_pg:2 * n_pg]
    k_refs = rest[2 * n_pg:3 * n_pg]
    v_refs = rest[3 * n_pg:4 * n_pg]
    (nkv_ref, nkr_ref, nk_ref, nv_ref, o_ref, osb_ref,
     m_ref, l_ref, acc_ref, carry_ref, sacc_ref) = rest[4 * n_pg:]
    g = pl.program_id(1)
    q = q_ref[...]
    qr = qr_ref[...]
    qbd = qbd_ref[...]
    page = nk_ref.shape[0] // kv_heads

    def heads_on_lanes(ref):
        return jnp.concatenate([ref[pl.ds(c, page, stride=kv_heads), :] for c in range(kv_heads)], axis=1).astype(BF)

    def sb_step(k, v, mask):
        z = _dot_nt(qbd, k) * sb_scale
        o, carry = _sb_block(z, mask, carry_ref[...], v)
        sacc_ref[...] += o
        carry_ref[...] = carry

    @pl.when(g == 0)
    def _():
        m_ref[...] = jnp.full(m_ref.shape, NEG_INF, F32)
        l_ref[...] = jnp.zeros(l_ref.shape, F32)
        acc_ref[...] = jnp.zeros(acc_ref.shape, F32)
        carry_ref[...] = jnp.zeros(carry_ref.shape, F32)
        sacc_ref[...] = jnp.zeros(sacc_ref.shape, F32)
        shape = (qbd.shape[0], page)
        row = lax.broadcasted_iota(jnp.int32, shape, 0)
        col = lax.broadcasted_iota(jnp.int32, shape, 1)
        sb_step(heads_on_lanes(nk_ref), heads_on_lanes(nv_ref), col < (row & (dec_seq - 1)))

    def update(s, kvs):
        m_prev = m_ref[...]
        m_new = jnp.maximum(m_prev, jnp.max(s, axis=-1, keepdims=True))
        alpha = jnp.exp(m_prev - m_new)
        p = jnp.exp(s - m_new)
        l_ref[...] = alpha * l_ref[...] + jnp.sum(p, axis=-1, keepdims=True)
        p = p.astype(BF)
        n = kvs[0].shape[0]
        pv = sum(_dot(p[:, t * n:(t + 1) * n], kvs[t]) for t in range(len(kvs)))
        acc_ref[...] = alpha * acc_ref[...] + pv
        m_ref[...] = m_new

    kvs = [r[...].astype(BF) for r in kv_refs]
    s = jnp.concatenate([_dot_nt(q, kvs[t]) + _dot(qr, kr_refs[t][...].astype(BF)) for t in range(n_pg)],
                        axis=1) * mla_scale
    update(s, kvs)

    k = jnp.concatenate([heads_on_lanes(k_refs[t]) for t in reversed(range(n_pg))], axis=0)
    v = jnp.concatenate([heads_on_lanes(v_refs[t]) for t in reversed(range(n_pg))], axis=0)
    o, carry = _sb_unmasked_blocks(_dot_nt(qbd, k) * sb_scale, carry_ref[...], v, 2 * page)
    sacc_ref[...] += o
    carry_ref[...] = carry

    @pl.when(g == pl.num_programs(1) - 1)
    def _():
        nkv = nkv_ref[...].astype(BF)
        sn = (_dot_nt(q, nkv) + _dot(qr, nkr_ref[...].astype(BF))) * mla_scale
        row = lax.broadcasted_iota(jnp.int32, sn.shape, 0)
        col = lax.broadcasted_iota(jnp.int32, sn.shape, 1)
        update(jnp.where(col <= (row & (dec_seq - 1)), sn, NEG_INF), [nkv])
        o_ref[...] = (acc_ref[...] * (1.0 / l_ref[...])).astype(o_ref.dtype)
        rows = sacc_ref.shape[0] // kv_heads
        dh = sacc_ref.shape[1] // kv_heads
        for c in range(kv_heads):
            osb_ref[c] = sacc_ref[c * rows:(c + 1) * rows, c * dh:(c + 1) * dh].astype(osb_ref.dtype)


def _decode(page_table, q_s, qr_s, q_bd, cache_kv, cache_kr_t, cache_k, cache_v,
            new_kv, new_kr_t, new_k, new_v, layer, dec_seq, kv_heads, mla_scale, sb_scale):
    db, rows, r = q_s.shape
    rows2, width = q_bd.shape[1:]
    dh = width // kv_heads
    n_pages = page_table.shape[1]
    page = cache_kv.shape[2]
    rope = cache_kr_t.shape[2]
    prow = cache_k.shape[2]
    n_pg = min(DECODE_PAGES, n_pages)
    assert n_pages % n_pg == 0 and n_pg % 2 == 0 and dec_seq & (dec_seq - 1) == 0

    def fwd_page(shape, t):
        return pl.BlockSpec((None, None) + shape, lambda b, g, pt: (layer, pt[b * n_pages + g * n_pg + t], 0, 0))

    def rev_page(t):
        return pl.BlockSpec((None, None, prow, dh),
                            lambda b, g, pt: (layer, pt[b * n_pages + n_pages - 1 - (g * n_pg + t)], 0, 0))

    def per_seq(shape):
        return pl.BlockSpec((None,) + shape, lambda b, g, pt: (b,) + (0,) * len(shape))

    in_specs = ([per_seq((rows, r)), per_seq((rows, rope)), per_seq((rows2, width))]
                + [fwd_page((page, r), t) for t in range(n_pg)]
                + [fwd_page((rope, page), t) for t in range(n_pg)]
                + [rev_page(t) for t in range(n_pg)] + [rev_page(t) for t in range(n_pg)]
                + [per_seq((page, r)), per_seq((rope, page)), per_seq((prow, dh)), per_seq((prow, dh))])
    return pl.pallas_call(
        functools.partial(_decode_kernel, n_pg=n_pg, dec_seq=dec_seq, kv_heads=kv_heads,
                          mla_scale=mla_scale, sb_scale=sb_scale),
        grid_spec=pltpu.PrefetchScalarGridSpec(
            num_scalar_prefetch=1,
            grid=(db, n_pages // n_pg),
            in_specs=in_specs,
            out_specs=(per_seq((rows, r)), per_seq((kv_heads, rows2 // kv_heads, dh))),
            scratch_shapes=[pltpu.VMEM((rows, 1), F32), pltpu.VMEM((rows, 1), F32), pltpu.VMEM((rows, r), F32),
                            pltpu.VMEM((rows2, 1), F32), pltpu.VMEM((rows2, width), F32)]),
        out_shape=(jax.ShapeDtypeStruct((db, rows, r), BF),
                   jax.ShapeDtypeStruct((db, kv_heads, rows2 // kv_heads, dh), BF)),
        compiler_params=_cp("parallel", "arbitrary"), name="decode",
    )(page_table.reshape(-1), q_s, qr_s, q_bd, *([cache_kv] * n_pg), *([cache_kr_t] * n_pg),
      *([cache_k] * n_pg), *([cache_v] * n_pg), new_kv, new_kr_t, new_k, new_v)


def _merge_kernel(vm_ref, os_ref, wm_ref, ws_ref, ga_ref, gb_ref, u_ref):
    ya = _dot(vm_ref[...], wm_ref[...].astype(BF))
    yb = _dot(os_ref[...], ws_ref[...].astype(BF))
    u_ref[...] = (ga_ref[...] * ya + gb_ref[...] * yb).astype(u_ref.dtype)


def _merge(v_mla, o_sb, w_branch_mla, w_branch_sb, gates, tm, tn):
    t, ka = v_mla.shape
    kb = o_sb.shape[1]
    d = w_branch_mla.shape[1]
    nj = d // tn
    return pl.pallas_call(
        _merge_kernel,
        grid=(t // tm, nj),
        in_specs=[pl.BlockSpec((tm, ka), lambda i, j: (i, 0)),
                  pl.BlockSpec((tm, kb), lambda i, j: (i, 0)),
                  pl.BlockSpec((ka, tn), lambda i, j: (0, j)),
                  pl.BlockSpec((kb, tn), lambda i, j: (0, j)),
                  pl.BlockSpec((tm, tn), lambda i, j: (i, j)),
                  pl.BlockSpec((tm, tn), lambda i, j: (i, nj + j))],
        out_specs=pl.BlockSpec((tm, tn), lambda i, j: (i, j)),
        out_shape=jax.ShapeDtypeStruct((t, d), BF),
        compiler_params=_cp("parallel", "arbitrary"), name="merge",
    )(v_mla, o_sb, w_branch_mla, w_branch_sb, gates, gates)


def _router_kernel(x_ref, g_ref, w_ref, b_ref, idx_ref, gate_ref, h_ref):
    h = _rms(x_ref[...]) * g_ref[...]
    h_ref[...] = h
    logits = jnp.dot(h, w_ref[...], preferred_element_type=F32, precision=lax.Precision.HIGHEST) + b_ref[...]
    n_exp = logits.shape[1]
    col = lax.broadcasted_iota(jnp.int32, logits.shape, 1)
    slot = lax.broadcasted_iota(jnp.int32, idx_ref.shape, 1)
    idx_out = jnp.zeros(idx_ref.shape, jnp.int32)
    val_out = jnp.zeros(gate_ref.shape, F32)
    top = None
    for k in range(TOP_K):
        m = jnp.max(logits, axis=-1, keepdims=True)
        idx = jnp.min(jnp.where(logits == m, col, n_exp), axis=-1, keepdims=True)
        top = m if top is None else top
        idx_out = jnp.where(slot == k, idx, idx_out)
        val_out = jnp.where(slot == k, jnp.exp(m - top), val_out)
        logits = jnp.where(col == idx, -jnp.inf, logits)
    idx_ref[...] = idx_out
    gate_ref[...] = val_out / jnp.sum(val_out, axis=-1, keepdims=True)


def _router(x, g, w_router, b_router, tm):
    t, d = x.shape
    n_exp = w_router.shape[1]
    return pl.pallas_call(
        _router_kernel,
        grid=(t // tm,),
        in_specs=[pl.BlockSpec((tm, d), lambda i: (i, 0)),
                  pl.BlockSpec((1, d), lambda i: (0, 0)),
                  pl.BlockSpec((d, n_exp), lambda i: (0, 0)),
                  pl.BlockSpec((1, n_exp), lambda i: (0, 0))],
        out_specs=(pl.BlockSpec((tm, TOP_K), lambda i: (i, 0)), pl.BlockSpec((tm, TOP_K), lambda i: (i, 0)),
                   pl.BlockSpec((tm, d), lambda i: (i, 0))),
        out_shape=(jax.ShapeDtypeStruct((t, TOP_K), jnp.int32), jax.ShapeDtypeStruct((t, TOP_K), F32),
                   jax.ShapeDtypeStruct((t, d), F32)),
        compiler_params=_cp("parallel"), name="router",
    )(x, g.reshape(1, d), w_router, b_router.reshape(1, n_exp))


ROW_CHUNK = 16


def _row_copy(src_hbm, row, dst, dst_row, sem):
    return pltpu.make_async_copy(src_hbm.at[pl.ds(row, 1), :], dst.at[pl.ds(dst_row, 1), :], sem)


def _start_rows(src_hbm, idx_ref, idx0, idx_stride, n, dst, dst_row0, sem):
    def issue(r2, c):
        for p in range(2):
            r = 2 * r2 + p
            _row_copy(src_hbm, idx_ref[idx0 + r * idx_stride], dst, dst_row0 + r, sem).start(priority=p)
        return c

    assert n % 2 == 0
    lax.fori_loop(0, n // 2, issue, 0, unroll=4)


def _wait_rows(src_hbm, dst, sem):
    pltpu.make_async_copy(src_hbm.at[pl.ds(0, dst.shape[0]), :], dst, sem).wait()


def _gather_rows_kernel(tok_ref, x_hbm, o_ref, buf, sem):
    tm = o_ref.shape[0]
    i = pl.program_id(0)
    slot = i % 2

    def start(block, s):
        _start_rows(x_hbm, tok_ref, block * tm, 1, tm, buf.at[s], 0, sem.at[s])

    @pl.when(i == 0)
    def _():
        start(0, 0)

    @pl.when(i + 1 < pl.num_programs(0))
    def _():
        start(i + 1, 1 - slot)

    _wait_rows(x_hbm, buf.at[slot], sem.at[slot])

    def chunk(c, carry):
        rows = pl.ds(pl.multiple_of(c * ROW_CHUNK, ROW_CHUNK), ROW_CHUNK)
        o_ref[rows, :] = buf[slot, rows, :].astype(o_ref.dtype)
        return carry

    lax.fori_loop(0, tm // ROW_CHUNK, chunk, 0, unroll=2)


def _gather_rows(x, row_tok, tm):
    t, d = x.shape
    n_rows = row_tok.shape[0]
    return pl.pallas_call(
        _gather_rows_kernel,
        grid_spec=pltpu.PrefetchScalarGridSpec(
            num_scalar_prefetch=1,
            grid=(n_rows // tm,),
            in_specs=[pl.BlockSpec(memory_space=pl.ANY)],
            out_specs=pl.BlockSpec((tm, d), lambda i, tok: (i, 0)),
            scratch_shapes=[pltpu.VMEM((2, tm, d), F32), pltpu.SemaphoreType.DMA((2,))]),
        out_shape=jax.ShapeDtypeStruct((n_rows, d), BF),
        compiler_params=_cp("arbitrary"), name="gather_rows",
    )(row_tok, x)


def _expert_rows(first_ref, count_ref, x_hbm, o_hbm, xbuf, obuf, sem_in, sem_out, compute):
    j, e = pl.program_id(0), pl.program_id(1)
    n_e = pl.num_programs(1)
    n = count_ref[e]
    row0 = first_ref[e]
    tm, tn = obuf.shape[1:]
    col0 = pl.multiple_of(j * tn, tn)

    def rows(first, b):
        return pl.ds(pl.multiple_of((first + b) * tm, tm), tm)

    def x_copy(first, b, slot):
        return pltpu.make_async_copy(x_hbm.at[rows(first, b), :], xbuf.at[slot], sem_in.at[slot])

    def o_copy(b, slot):
        return pltpu.make_async_copy(obuf.at[slot], o_hbm.at[rows(row0, b), pl.ds(col0, tn)], sem_out.at[slot])

    def start_head(first, count):
        for b in range(2):
            @pl.when(count > b)
            def _():
                x_copy(first, b, b).start()

    @pl.when((j == 0) & (e == 0))
    def _():
        start_head(row0, n)

    def body(b, carry):
        slot = b % 2
        x_copy(row0, b, slot).wait()

        @pl.when(b >= 2)
        def _():
            o_copy(b - 2, slot).wait()

        obuf[slot] = compute(xbuf[slot])
        o_copy(b, slot).start()

        @pl.when(b + 2 < n)
        def _():
            x_copy(row0, b + 2, slot).start()

        return carry

    lax.fori_loop(0, n, body, 0)

    @pl.when(n >= 2)
    def _():
        o_copy(n - 2, n % 2).wait()

    @pl.when(n >= 1)
    def _():
        o_copy(n - 1, (n - 1) % 2).wait()

    @pl.when((j + 1 < pl.num_programs(0)) | (e + 1 < n_e))
    def _():
        e_next = jnp.where(e + 1 < n_e, e + 1, 0)
        start_head(first_ref[e_next], count_ref[e_next])


def _expert_up_kernel(first_ref, count_ref, x_hbm, wg_ref, wu_ref, bg_ref, bu_ref, h_hbm, xbuf, hbuf, sem_in, sem_out):
    def compute(x):
        g = _dot(x, wg_ref[...].astype(BF)) + bg_ref[...]
        u = _dot(x, wu_ref[...].astype(BF)) + bu_ref[...]
        g = jnp.minimum(g, SWIGLU_LIMIT)
        u = jnp.clip(u, -SWIGLU_LIMIT, SWIGLU_LIMIT)
        return (g * jax.nn.sigmoid(SWIGLU_ALPHA * g) * (u + 1.0)).astype(hbuf.dtype)

    _expert_rows(first_ref, count_ref, x_hbm, h_hbm, xbuf, hbuf, sem_in, sem_out, compute)


def _expert_up(xs, first_block, n_blocks, w_gate_up, b_gate_up, layer, tm, tn):
    n_rows, d = xs.shape
    d_ff = w_gate_up.shape[3] // 2
    nj = d_ff // tn
    n_exp = w_gate_up.shape[1]
    b3 = b_gate_up.reshape(b_gate_up.shape[0], n_exp, 1, 2 * d_ff)
    return pl.pallas_call(
        _expert_up_kernel,
        grid_spec=pltpu.PrefetchScalarGridSpec(
            num_scalar_prefetch=2,
            grid=(nj, n_exp),
            in_specs=[pl.BlockSpec(memory_space=pl.ANY),
                      pl.BlockSpec((None, None, d, tn), lambda j, e, fb, nb: (layer, e, 0, j)),
                      pl.BlockSpec((None, None, d, tn), lambda j, e, fb, nb: (layer, e, 0, nj + j)),
                      pl.BlockSpec((None, None, 1, tn), lambda j, e, fb, nb: (layer, e, 0, j)),
                      pl.BlockSpec((None, None, 1, tn), lambda j, e, fb, nb: (layer, e, 0, nj + j))],
            out_specs=pl.BlockSpec(memory_space=pl.ANY),
            scratch_shapes=[pltpu.VMEM((2, tm, d), BF), pltpu.VMEM((2, tm, tn), BF),
                            pltpu.SemaphoreType.DMA((2,)), pltpu.SemaphoreType.DMA((2,))]),
        out_shape=jax.ShapeDtypeStruct((n_rows, d_ff), BF),
        compiler_params=_cp("arbitrary", "arbitrary"), name="expert_up",
    )(first_block, n_blocks, xs, w_gate_up, w_gate_up, b3, b3)


def _expert_down_kernel(first_ref, count_ref, h_hbm, w_ref, b_ref, y_hbm, hbuf, ybuf, sem_in, sem_out):
    def compute(h):
        return _dot(h, w_ref[...].astype(BF)) + b_ref[...]

    _expert_rows(first_ref, count_ref, h_hbm, y_hbm, hbuf, ybuf, sem_in, sem_out, compute)


def _expert_down(hs, first_block, n_blocks, w_down, b_down, layer, tm, tn):
    n_rows, d_ff = hs.shape
    n_exp, _, d = w_down.shape[1:]
    b3 = b_down.reshape(b_down.shape[0], n_exp, 1, d)
    return pl.pallas_call(
        _expert_down_kernel,
        grid_spec=pltpu.PrefetchScalarGridSpec(
            num_scalar_prefetch=2,
            grid=(d // tn, n_exp),
            in_specs=[pl.BlockSpec(memory_space=pl.ANY),
                      pl.BlockSpec((None, None, d_ff, tn), lambda j, e, fb, nb: (layer, e, 0, j)),
                      pl.BlockSpec((None, None, 1, tn), lambda j, e, fb, nb: (layer, e, 0, j))],
            out_specs=pl.BlockSpec(memory_space=pl.ANY),
            scratch_shapes=[pltpu.VMEM((2, tm, d_ff), BF), pltpu.VMEM((2, tm, tn), F32),
                            pltpu.SemaphoreType.DMA((2,)), pltpu.SemaphoreType.DMA((2,))]),
        out_shape=jax.ShapeDtypeStruct((n_rows, d), F32),
        compiler_params=_cp("arbitrary", "arbitrary"), name="expert_down",
    )(first_block, n_blocks, hs, w_down, b3)


def _combine_kernel(dest_ref, y_hbm, gate_ref, x_ref, g_ref, op_ref, os_ref, buf, sem, *, n_first):
    tc = x_ref.shape[0]
    i = pl.program_id(0)
    slot = i % 2

    def start(block, s):
        for k in range(TOP_K):
            _start_rows(y_hbm, dest_ref, block * tc * TOP_K + k, TOP_K, tc, buf.at[s], k * tc, sem.at[s])

    @pl.when(i == 0)
    def _():
        start(0, 0)

    @pl.when(i + 1 < pl.num_programs(0))
    def _():
        start(i + 1, 1 - slot)

    _wait_rows(y_hbm, buf.at[slot], sem.at[slot])

    def write(o_ref):
        def chunk(c, carry):
            rows = pl.ds(pl.multiple_of(c * 8, 8), 8)
            gate = gate_ref[rows, :]
            moe = sum(buf[slot, pl.ds(pl.multiple_of(k * tc + c * 8, 8), 8), :] * gate[:, k:k + 1]
                      for k in range(TOP_K))
            o_ref[rows, :] = _rms(x_ref[rows, :] + moe) * g_ref[...]
            return carry

        lax.fori_loop(0, tc // 8, chunk, 0, unroll=2)

    @pl.when(i < n_first)
    def _():
        write(op_ref)

    @pl.when(i >= n_first)
    def _():
        write(os_ref)


def _combine(y_rows, dest, gate, x, g_final, tc, t_first):
    t, d = x.shape
    n_first = t_first // tc
    assert t_first % tc == 0 and 0 < n_first < t // tc
    return pl.pallas_call(
        functools.partial(_combine_kernel, n_first=n_first),
        grid_spec=pltpu.PrefetchScalarGridSpec(
            num_scalar_prefetch=1,
            grid=(t // tc,),
            in_specs=[pl.BlockSpec(memory_space=pl.ANY),
                      pl.BlockSpec((tc, TOP_K), lambda i, dst: (i, 0)),
                      pl.BlockSpec((tc, d), lambda i, dst: (i, 0)),
                      pl.BlockSpec((1, d), lambda i, dst: (0, 0))],
            out_specs=(pl.BlockSpec((tc, d), lambda i, dst: (jnp.minimum(i, n_first - 1), 0)),
                       pl.BlockSpec((tc, d), lambda i, dst: (jnp.maximum(i - n_first, 0), 0))),
            scratch_shapes=[pltpu.VMEM((2, TOP_K * tc, d), F32), pltpu.SemaphoreType.DMA((2,))]),
        out_shape=(jax.ShapeDtypeStruct((t_first, d), F32), jax.ShapeDtypeStruct((t - t_first, d), F32)),
        compiler_params=_cp("arbitrary"), name="combine",
    )(dest, y_rows, gate, x, g_final.reshape(1, d))


def _routing_tables(top_idx, n_exp, tm):
    n_pairs = top_idx.size
    flat_e = top_idx.reshape(-1)
    onehot = (flat_e[:, None] == jnp.arange(n_exp, dtype=jnp.int32)[None, :]).astype(jnp.int32)
    csum = jnp.cumsum(onehot, axis=0)
    counts = csum[-1]
    padded = (counts + tm - 1) // tm * tm
    pad_ends = jnp.cumsum(padded)
    pad_starts = pad_ends - padded
    dest = jnp.sum(onehot * (pad_starts[None, :] + csum - 1), axis=1).astype(jnp.int32)
    n_rows = (n_pairs + n_exp * (tm - 1) + tm - 1) // tm * tm
    row_tok = (jnp.arange(n_rows, dtype=jnp.int32) % (n_pairs // TOP_K)).at[dest].set(
        jnp.arange(n_pairs, dtype=jnp.int32) // TOP_K)
    return dest, row_tok, (pad_starts // tm).astype(jnp.int32), (padded // tm).astype(jnp.int32)


def _pick(n, pref):
    b = min(pref, n)
    while n % b:
        b //= 2
    return b


def kernel(x_prompt, x_sample, cache_kv_latent, cache_k_rope, cache_sb_k, cache_sb_v, page_table, norm_attn, w_in, b_gate, q_a_norm, w_uq, kv_a_norm, w_ukv, w_branch_mla, w_branch_sb, w_out, norm_ffn, w_router, b_router, w_gate_up, b_gate_up, w_down, b_down, norm_final):
    batch, seq, d = x_prompt.shape
    db, dec_seq, _ = x_sample.shape
    depth = w_in.shape[0]
    q_rank = q_a_norm.shape[1]
    kv_rank = kv_a_norm.shape[1]
    rope = cache_k_rope.shape[3]
    heads = w_ukv.shape[2]
    nope = w_uq.shape[2] // heads - rope
    page = cache_kv_latent.shape[2]
    kv_heads, dh = cache_sb_k.shape[3], cache_sb_k.shape[4]
    n_pages = page_table.shape[1]
    past_len = n_pages * page
    in_cols = w_in.shape[2]
    sb_heads = (in_cols - q_rank - kv_rank - rope - 2 * d) // dh - 2 * kv_heads
    group = sb_heads // kv_heads
    n_exp = w_router.shape[2]
    mla_scale = float(nope + rope) ** -0.5
    sb_scale = float(dh) ** -0.5
    assert rope == 64 and dh == LANES and kv_rank % LANES == 0 and (q_rank + kv_rank) % LANES == 0

    tp, ts = batch * seq, db * dec_seq
    t = tp + ts
    x = jnp.concatenate([x_prompt.reshape(tp, d), x_sample.reshape(ts, d)], axis=0)

    pos = jnp.concatenate([jnp.tile(jnp.arange(seq), batch), jnp.tile(past_len + jnp.arange(dec_seq), db)])
    inv_freq = 1.0 / (ROPE_THETA ** (jnp.arange(0, rope, 2, dtype=F32) / rope))
    ang = pos.astype(F32)[:, None] * inv_freq[None, :]
    cos, sin = jnp.cos(ang), jnp.sin(ang)
    cosf = jnp.concatenate([cos, cos, cos, cos], axis=1)
    sinf = jnp.concatenate([-sin, sin, -sin, sin], axis=1)

    tm_big = _pick(t, 1024)
    tm_mid = _pick(t, 512)
    outs_p, outs_s = [], []
    for l in range(depth):
        rest0 = q_rank + kv_rank + rope
        w_in_t = jnp.swapaxes(w_in, 1, 2)[l]
        w_rest = w_in_t[rest0:]
        w_uq3 = w_uq[l].reshape(q_rank, heads, nope + rope)
        w_nope = jnp.transpose(w_uq3[:, :, :nope], (1, 0, 2))
        w_rope_pair = jnp.transpose(w_uq3[:, :, nope:].reshape(q_rank, heads // 2, 2 * rope), (1, 0, 2))
        w_kt = jnp.transpose(w_ukv[l][:, :, :nope], (1, 2, 0))
        w_v = jnp.transpose(w_ukv[l][:, :, nope:], (1, 0, 2))

        h = _rmsnorm(x, norm_attn[l], BF, _pick(t, 256))
        cq, ckv = _latents(h, w_in_t, q_a_norm[l], kv_a_norm[l], q_rank, kv_rank, tm_mid)
        kr, krd = _rope_key(h, w_in_t, (q_rank + kv_rank) // LANES, cosf, sinf, rope, tm_big)
        c0 = sb_heads * dh
        tn = _pick(d, 512)
        assert c0 % tn == 0 and (c0 + 2 * kv_heads * dh) % tn == 0 and c0 % (kv_heads * dh) == 0
        sb_q = _plain_proj(h, w_rest, 0, c0 // tn, tm_big, tn, BF)
        sb_kv = _plain_proj(h, w_rest, c0 // (kv_heads * dh), 2, tm_big, kv_heads * dh, F32)
        sb_k, sb_v = sb_kv[:, :kv_heads * dh], sb_kv[:, kv_heads * dh:]
        gates = _gates(h, w_rest, (c0 + 2 * kv_heads * dh) // tn, b_gate[l], tm_big, tn)
        q_lat, q_rope = _mla_queries(cq, w_nope, w_rope_pair, w_kt, cosf, sinf, tm_big)

        o_lat_p = _mla_prompt(q_lat, q_rope, ckv, krd, batch, seq, min(128, seq), min(512, seq), mla_scale)
        o_sb_p = _sb_prompt(sb_q, sb_k, sb_v, batch, seq, kv_heads, min(128, seq), min(256, seq), sb_scale)

        def seq_major(a):
            return jnp.transpose(a[:, tp:].reshape(a.shape[0], db, dec_seq, a.shape[2]), (1, 0, 2, 3)).reshape(
                db, a.shape[0] * dec_seq, a.shape[2])

        def new_page(a, rows_per_pos):
            a = a[tp:].reshape(db, dec_seq * rows_per_pos, a.shape[1] // rows_per_pos)
            return jnp.pad(a, ((0, 0), (0, (page - dec_seq) * rows_per_pos), (0, 0)))

        q_s = seq_major(q_lat)
        qr_m = seq_major(q_rope)
        qr_s = qr_m[:, :, :rope] + qr_m[:, :, rope:]
        sbq_s = jnp.transpose(sb_q[tp:].reshape(db, dec_seq, kv_heads, group, dh), (0, 2, 3, 1, 4)).reshape(
            db, kv_heads, group * dec_seq, dh)
        q_bd = jnp.concatenate(
            [jnp.concatenate([sbq_s[:, c] if c2 == c else jnp.zeros_like(sbq_s[:, c]) for c2 in range(kv_heads)], axis=2)
             for c in range(kv_heads)], axis=1)
        cache_k2 = cache_sb_k.reshape(depth, -1, page * kv_heads, dh)
        cache_v2 = cache_sb_v.reshape(depth, -1, page * kv_heads, dh)
        cache_kr_t = jnp.swapaxes(cache_k_rope, 2, 3)
        o_lat_s, o_sb_s = _decode(page_table, q_s, qr_s, q_bd, cache_kv_latent, cache_kr_t, cache_k2, cache_v2,
                                  new_page(ckv, 1), jnp.swapaxes(new_page(kr, 1), 1, 2),
                                  new_page(sb_k, kv_heads), new_page(sb_v, kv_heads),
                                  l, dec_seq, kv_heads, mla_scale, sb_scale)
        o_lat_s = jnp.transpose(o_lat_s.reshape(db, heads, dec_seq, kv_rank), (1, 0, 2, 3)).reshape(heads, ts, kv_rank)
        o_sb_s = jnp.transpose(o_sb_s.reshape(db, kv_heads, group, dec_seq, dh), (0, 3, 1, 2, 4)).reshape(ts, sb_heads * dh)

        v_mla = jnp.concatenate([_value_up(o_lat_p, w_v, _pick(tp, 1024)), _value_up(o_lat_s, w_v, _pick(ts, 1024))], axis=0)
        o_sb = jnp.concatenate([o_sb_p, o_sb_s], axis=0)
        u = _merge(v_mla, o_sb, w_branch_mla[l], w_branch_sb[l], gates, tm_big, tn)
        x = _out_proj(u, w_out[l], x, tm_big, tn)

        top_idx, gate, h_ffn = _router(x, norm_ffn[l], w_router[l], b_router[l], _pick(t, 256))
        dest, row_tok, first_block, n_blocks = _routing_tables(top_idx, n_exp, MOE_BLOCK)
        xs = _gather_rows(h_ffn, row_tok, MOE_BLOCK)
        hs = _expert_up(xs, first_block, n_blocks, w_gate_up, b_gate_up, l, MOE_BLOCK, _pick(w_gate_up.shape[3] // 2, 512))
        ys = _expert_down(hs, first_block, n_blocks, w_down, b_down, l, MOE_BLOCK, _pick(d, 1024))
        last = l == depth - 1
        assert last, "a deeper stack needs a combine variant without the final norm"
        y_p, y_s = _combine(ys, dest, gate, x, norm_final, _pick(ts, 64), tp)

        outs_p.append((ckv[:tp].reshape(batch, seq, kv_rank), kr[:tp].reshape(batch, seq, rope),
                       sb_k[:tp].reshape(batch, seq, kv_heads, dh), sb_v[:tp].reshape(batch, seq, kv_heads, dh)))
        outs_s.append((ckv[tp:].reshape(db, dec_seq, kv_rank), kr[tp:].reshape(db, dec_seq, rope),
                       sb_k[tp:].reshape(db, dec_seq, kv_heads, dh), sb_v[tp:].reshape(db, dec_seq, kv_heads, dh)))

    y_prompt = y_p.reshape(batch, seq, d)
    y_sample = y_s.reshape(db, dec_seq, d)
    stack = lambda rows, k: jnp.stack([r[k] for r in rows])
    return (y_prompt, y_sample) + tuple(stack(outs_p, k) for k in range(4)) + tuple(stack(outs_s, k) for k in range(4))
```
